```python
import numpy as np
import jax, jax.numpy as jnp
from jax import lax

D_MODEL = 2048
BATCH = 2
SEQ = 4096
DEPTH = 1
DEC_BATCH = 8
DEC_SEQ = 2048
PAST_LEN = 128

HEAD_DIM = 128
N_HEADS_A = 8
N_KV_HEADS_A = 2
N_HEADS_B = 8
GRID_W = 64
WIN_ROWS_MAX = 8
WIN_COLS = 16
COL_SEG = 16
KEY_COLS = 32
Q_BLOCK = 128
D_FF = 5504
ROPE_THETA = 10000.0
EPS = 1e-6
RPB_STD = 0.02

WIDTH_QA = N_HEADS_A * HEAD_DIM
WIDTH_KVA = N_KV_HEADS_A * HEAD_DIM
WIDTH_B = N_HEADS_B * HEAD_DIM
IN_SPLITS = (WIDTH_QA, WIDTH_KVA, WIDTH_KVA, WIDTH_B, WIDTH_B, WIDTH_B, D_MODEL, D_MODEL)
IN_WIDTH = sum(IN_SPLITS)
IN_OFFSETS = tuple(int(o) for o in np.cumsum(IN_SPLITS)[:-1])

kernel_name = "hybrid_gqa_natten_macaron_encoder"


def rms_norm(x, g):
    xf = x.astype(jnp.float32)
    y = xf * lax.rsqrt(jnp.mean(xf * xf, axis=-1, keepdims=True) + EPS)
    return (y * g.astype(jnp.float32)).astype(x.dtype)


def swiglu(x, w_in, w_out):
    gate, up = jnp.split(x @ w_in, 2, axis=-1)
    return (jax.nn.silu(gate) * up) @ w_out


def axial_rope_tables(T):
    t = np.arange(T)
    row = (t // GRID_W).astype(np.float32)
    col = (t % GRID_W).astype(np.float32)
    n_pairs_axis = HEAD_DIM // 4
    inv = (ROPE_THETA ** (-np.arange(n_pairs_axis, dtype=np.float32) / n_pairs_axis)).astype(np.float32)
    ang = np.concatenate([row[:, None] * inv[None], col[:, None] * inv[None]], axis=-1)
    return jnp.asarray(np.cos(ang), jnp.float32), jnp.asarray(np.sin(ang), jnp.float32)


def apply_rope(x, cos, sin):
    xf = x.astype(jnp.float32).reshape(x.shape[:-1] + (HEAD_DIM // 2, 2))
    x0, x1 = xf[..., 0], xf[..., 1]
    c, s = cos[:, None, :], sin[:, None, :]
    out = jnp.stack([x0 * c - x1 * s, x0 * s + x1 * c], axis=-1)
    return out.reshape(x.shape).astype(x.dtype)


def global_gqa(q, k, v, g_q, g_k):
    B, T = q.shape[0], q.shape[1]
    cos, sin = axial_rope_tables(T)
    q = apply_rope(rms_norm(q, g_q), cos, sin)
    k = apply_rope(rms_norm(k, g_k), cos, sin)
    G = N_HEADS_A // N_KV_HEADS_A
    nb = T // Q_BLOCK
    qb = q.reshape(B, nb, Q_BLOCK, N_KV_HEADS_A, G, HEAD_DIM).transpose(1, 0, 2, 3, 4, 5)
    scale = HEAD_DIM ** -0.5

    def block(qi):
        s = jnp.einsum('bqkgd,bskd->bkgqs', qi, k, preferred_element_type=jnp.float32) * scale
        p = jax.nn.softmax(s, axis=-1).astype(v.dtype)
        return jnp.einsum('bkgqs,bskd->bqkgd', p, v)

    o = lax.map(block, qb)
    return o.transpose(1, 0, 2, 3, 4, 5).reshape(B, T, WIDTH_QA)


def neighbourhood_attn(q, k, v, rpb):
    B, T, H, hd = q.shape
    rows = T // GRID_W
    wr = min(WIN_ROWS_MAX, rows)
    n_seg = GRID_W // COL_SEG
    qc = np.arange(GRID_W).reshape(n_seg, COL_SEG)
    seg_start = np.clip(qc[:, 0] - WIN_COLS // 2, 0, GRID_W - KEY_COLS)
    kc = seg_start[:, None] + np.arange(KEY_COLS)[None, :]
    cs = np.clip(qc - WIN_COLS // 2, 0, GRID_W - WIN_COLS)
    valid = (kc[:, None, :] >= cs[:, :, None]) & (kc[:, None, :] < cs[:, :, None] + WIN_COLS)
    dc_idx = np.clip(kc[:, None, :] - qc[:, :, None] + WIN_COLS - 1, 0, 2 * WIN_COLS - 2)
    mask_add = jnp.asarray(np.where(valid, 0.0, -1e30).astype(np.float32))[:, :, None, :]
    qg = q.reshape(B, rows, n_seg, COL_SEG, H, hd)
    kg = k.reshape(B, rows, GRID_W, H, hd)
    vg = v.reshape(B, rows, GRID_W, H, hd)
    scale = hd ** -0.5
    rpb_f = rpb.astype(jnp.float32)

    def row_block(r):
        rs = jnp.clip(r - wr // 2, 0, rows - wr)
        kb = lax.dynamic_slice_in_dim(kg, rs, wr, axis=1)[:, :, kc]
        vb = lax.dynamic_slice_in_dim(vg, rs, wr, axis=1)[:, :, kc]
        qr = lax.dynamic_index_in_dim(qg, r, axis=1, keepdims=False)
        s = jnp.einsum('bnqhd,bwnkhd->bhnqwk', qr, kb, preferred_element_type=jnp.float32) * scale
        dr_idx = rs + jnp.arange(wr) - r + WIN_ROWS_MAX - 1
        bias = jnp.take(rpb_f, dr_idx, axis=1)[:, :, dc_idx]
        s = s + bias.transpose(0, 2, 3, 1, 4) + mask_add
        p = jax.nn.softmax(s.reshape(s.shape[:4] + (wr * KEY_COLS,)), axis=-1).reshape(s.shape)
        o = jnp.einsum('bhnqwk,bwnkhd->bnqhd', p.astype(vb.dtype), vb)
        return o.reshape(B, GRID_W, H * hd)

    o = lax.map(row_block, jnp.arange(rows))
    return o.transpose(1, 0, 2, 3).reshape(B, T, H * hd)


def encoder_layer(x, g_ffn1, w_ffn1_in, w_ffn1_out, g_mix, w_in, b_gate, g_q_a, g_k_a, rpb_b,
                  w_branch_a, w_branch_b, w_out, g_ffn2, w_ffn2_in, w_ffn2_out):
    B, T, _ = x.shape
    h = x + 0.5 * swiglu(rms_norm(x, g_ffn1), w_ffn1_in, w_ffn1_out)
    u = rms_norm(h, g_mix)
    qa, ka, va, qb, kb, vb, gate_a, gate_b = jnp.split(u @ w_in, IN_OFFSETS, axis=-1)
    ya = global_gqa(qa.reshape(B, T, N_HEADS_A, HEAD_DIM), ka.reshape(B, T, N_KV_HEADS_A, HEAD_DIM),
                    va.reshape(B, T, N_KV_HEADS_A, HEAD_DIM), g_q_a, g_k_a)
    yb = neighbourhood_attn(qb.reshape(B, T, N_HEADS_B, HEAD_DIM), kb.reshape(B, T, N_HEADS_B, HEAD_DIM),
                            vb.reshape(B, T, N_HEADS_B, HEAD_DIM), rpb_b)
    ga, gb = jnp.split(b_gate, 2)
    merged = jax.nn.sigmoid(gate_a + ga) * (ya @ w_branch_a) + jax.nn.sigmoid(gate_b + gb) * (yb @ w_branch_b)
    h = h + merged @ w_out
    h = h + 0.5 * swiglu(rms_norm(h, g_ffn2), w_ffn2_in, w_ffn2_out)
    return h


def setup_inputs(seed: int = 0) -> dict:
    key = jax.random.key(seed)
    ks = jax.random.split(key, 20)
    f32 = jnp.float32

    def w(k, shape, fan_in):
        return jax.random.normal(k, shape, f32) * (fan_in ** -0.5)

    def gain(k, shape):
        return 1.0 + 0.02 * jax.random.normal(k, shape, f32)

    return {
        "x_prompt": jax.random.normal(ks[0], (BATCH, SEQ, D_MODEL), f32),
        "x_sample": jax.random.normal(ks[1], (DEC_BATCH, DEC_SEQ, D_MODEL), f32),
        "g_ffn1": gain(ks[2], (DEPTH, D_MODEL)),
        "w_ffn1_in": w(ks[3], (DEPTH, D_MODEL, 2 * D_FF), D_MODEL),
        "w_ffn1_out": w(ks[4], (DEPTH, D_FF, D_MODEL), D_FF),
        "g_mix": gain(ks[5], (DEPTH, D_MODEL)),
        "w_in": w(ks[6], (DEPTH, D_MODEL, IN_WIDTH), D_MODEL),
        "b_gate": 0.02 * jax.random.normal(ks[7], (DEPTH, 2 * D_MODEL), f32),
        "g_q_a": gain(ks[8], (DEPTH, HEAD_DIM)),
        "g_k_a": gain(ks[9], (DEPTH, HEAD_DIM)),
        "rpb_b": RPB_STD * jax.random.normal(ks[10], (DEPTH, N_HEADS_B, 2 * WIN_ROWS_MAX - 1, 2 * WIN_COLS - 1), f32),
        "w_branch_a": w(ks[11], (DEPTH, WIDTH_QA, D_MODEL), WIDTH_QA),
        "w_branch_b": w(ks[12], (DEPTH, WIDTH_B, D_MODEL), WIDTH_B),
        "w_out": w(ks[13], (DEPTH, D_MODEL, D_MODEL), D_MODEL),
        "g_ffn2": gain(ks[14], (DEPTH, D_MODEL)),
        "w_ffn2_in": w(ks[15], (DEPTH, D_MODEL, 2 * D_FF), D_MODEL),
        "w_ffn2_out": w(ks[16], (DEPTH, D_FF, D_MODEL), D_FF),
        "g_final": gain(ks[17], (D_MODEL,)),
    }


def reference(x_prompt, x_sample, g_ffn1, w_ffn1_in, w_ffn1_out, g_mix, w_in, b_gate, g_q_a, g_k_a,
              rpb_b, w_branch_a, w_branch_b, w_out, g_ffn2, w_ffn2_in, w_ffn2_out, g_final):
    def trunk(x):
        h = x
        for l in range(DEPTH):
            h = encoder_layer(h, g_ffn1[l], w_ffn1_in[l], w_ffn1_out[l], g_mix[l], w_in[l], b_gate[l],
                              g_q_a[l], g_k_a[l], rpb_b[l], w_branch_a[l], w_branch_b[l], w_out[l],
                              g_ffn2[l], w_ffn2_in[l], w_ffn2_out[l])
        return rms_norm(h, g_final)

    y_prompt = trunk(x_prompt)
    y_sample = trunk(x_sample)
    return (y_prompt, y_sample)
```

```python
import functools

import numpy as np
import jax
import jax.numpy as jnp
from jax import lax
from jax.experimental import pallas as pl
from jax.experimental.pallas import tpu as pltpu

D_MODEL = 2048
HEAD_DIM = 128
N_HEADS_A = 8
N_KV_HEADS_A = 2
GROUP_A = N_HEADS_A // N_KV_HEADS_A
N_HEADS_B = 8
GRID_W = 64
WIN_ROWS = 8
WIN_COLS = 16
D_FF = 5504
ROPE_THETA = 10000.0
EPS = 1e-6
NEG = -1e30

WIDTH_QA = N_HEADS_A * HEAD_DIM
WIDTH_KVA = N_KV_HEADS_A * HEAD_DIM
WIDTH_B = N_HEADS_B * HEAD_DIM
QKV_WIDTH = WIDTH_QA + 2 * WIDTH_KVA + 3 * WIDTH_B
GATE_WIDTH = 2 * D_MODEL

CB_QA = 0
CB_KA = WIDTH_QA // HEAD_DIM
CB_VA = CB_KA + N_KV_HEADS_A
CB_QB = CB_VA + N_KV_HEADS_A
CB_KB = CB_QB + N_HEADS_B
CB_VB = CB_KB + N_HEADS_B

V7X_VMEM_BYTES = 64 * 1024 * 1024
VMEM_LIMIT = V7X_VMEM_BYTES - 8 * 1024 * 1024

FF_TILE = 512
D_FF_PAD = -(-D_FF // FF_TILE) * FF_TILE
FFN_TM = 512
PROJ_TM = 2048
PROJ_TN = 512
MERGE_TM = 256
GQA_TQ = 128
GQA_TK = 512
NBR_ROWS = 8
NBR_WIN = NBR_ROWS + WIN_ROWS
NBR_TQ = NBR_ROWS * GRID_W
NBR_TK = NBR_WIN * GRID_W

F32 = jnp.float32
BF16 = jnp.bfloat16


def _params(n_axes):
    return pltpu.CompilerParams(dimension_semantics=("arbitrary",) * n_axes,
                                vmem_limit_bytes=VMEM_LIMIT)


def _rms(xf, g):
    ms = jnp.mean(xf * xf, axis=-1, keepdims=True)
    return xf * lax.rsqrt(ms + EPS) * g


def _ffn_kernel(x_ref, g_ref, wg_ref, wu_ref, wo_ref, g2_ref, *refs, final):
    if final:
        o_ref, xn_ref, acc_ref = refs
    else:
        h_ref, u_ref, xn_ref, acc_ref = refs
    j = pl.program_id(1)

    @pl.when(j == 0)
    def _():
        xn_ref[...] = _rms(x_ref[...], g_ref[...]).astype(BF16)
        acc_ref[...] = jnp.zeros_like(acc_ref)

    xn = xn_ref[...]
    gate = jnp.dot(xn, wg_ref[...], preferred_element_type=F32)
    up = jnp.dot(xn, wu_ref[...], preferred_element_type=F32)
    act = ((gate * jax.nn.sigmoid(gate)) * up).astype(BF16)
    acc_ref[...] += jnp.dot(act, wo_ref[...], preferred_element_type=F32)

    @pl.when(j == pl.num_programs(1) - 1)
    def _():
        h = x_ref[...] + 0.5 * acc_ref[...]
        if final:
            o_ref[...] = _rms(h, g2_ref[...])
        else:
            h_ref[...] = h
            u_ref[...] = _rms(h, g2_ref[...]).astype(BF16)


def _ffn(x, g, wg, wu, wo, g2, *, final):
    n = x.shape[0]
    tm = FFN_TM
    nf = D_FF_PAD // FF_TILE
    row = lambda i, j: (i, 0)
    in_specs = [
        pl.BlockSpec((tm, D_MODEL), row),
        pl.BlockSpec((1, D_MODEL), lambda i, j: (0, 0)),
        pl.BlockSpec((D_MODEL, FF_TILE), lambda i, j: (0, j)),
        pl.BlockSpec((D_MODEL, FF_TILE), lambda i, j: (0, j)),
        pl.BlockSpec((FF_TILE, D_MODEL), lambda i, j: (j, 0)),
        pl.BlockSpec((1, D_MODEL), lambda i, j: (0, 0)),
    ]
    if final:
        out_shape = jax.ShapeDtypeStruct((n, D_MODEL), F32)
        out_specs = pl.BlockSpec((tm, D_MODEL), row)
    else:
        out_shape = (jax.ShapeDtypeStruct((n, D_MODEL), F32), jax.ShapeDtypeStruct((n, D_MODEL), BF16))
        out_specs = (pl.BlockSpec((tm, D_MODEL), row), pl.BlockSpec((tm, D_MODEL), row))
    return pl.pallas_call(
        functools.partial(_ffn_kernel, final=final),
        grid=(n // tm, nf),
        in_specs=in_specs,
        out_specs=out_specs,
        out_shape=out_shape,
        scratch_shapes=[pltpu.VMEM((tm, D_MODEL), BF16), pltpu.VMEM((tm, D_MODEL), F32)],
        compiler_params=_params(2),
        name="ffn_final" if final else "ffn_first",
    )(x, g, wg, wu, wo, g2)


def _rope(y, cos2, sin2):
    lane = lax.broadcasted_iota(jnp.int32, y.shape, 1)
    partner = jnp.where(lane % 2 == 0, pltpu.roll(y, HEAD_DIM - 1, 1), pltpu.roll(y, 1, 1))
    return y * cos2 + partner * sin2


def _proj_kernel(u_ref, w_ref, gq_ref, gk_ref, cos_ref, sin_ref, o_ref):
    j = pl.program_id(1)
    y = jnp.dot(u_ref[...], w_ref[...], preferred_element_type=F32)
    scale = HEAD_DIM ** -0.5
    heads_per_tile = PROJ_TN // HEAD_DIM
    qa_tiles = WIDTH_QA // PROJ_TN
    kva_tile = qa_tiles
    qb_lo = (CB_QB * HEAD_DIM) // PROJ_TN
    qb_hi = (CB_KB * HEAD_DIM) // PROJ_TN

    def normed_rope(yh, g_ref):
        return _rope(_rms(yh, g_ref[...]), cos_ref[...], sin_ref[...])

    @pl.when(j < qa_tiles)
    def _():
        for hh in range(heads_per_tile):
            sl = slice(hh * HEAD_DIM, (hh + 1) * HEAD_DIM)
            o_ref[:, sl] = (normed_rope(y[:, sl], gq_ref) * scale).astype(BF16)

    @pl.when(j == kva_tile)
    def _():
        for hh in range(N_KV_HEADS_A):
            sl = slice(hh * HEAD_DIM, (hh + 1) * HEAD_DIM)
            o_ref[:, sl] = normed_rope(y[:, sl], gk_ref).astype(BF16)
        o_ref[:, WIDTH_KVA:] = y[:, WIDTH_KVA:].astype(BF16)

    @pl.when((j >= qb_lo) & (j < qb_hi))
    def _():
        o_ref[...] = (y * scale).astype(BF16)

    @pl.when(j >= qb_hi)
    def _():
        o_ref[...] = y.astype(BF16)


def _proj(u, w_qkv, gq, gk, cos2, sin2, seq):
    n = u.shape[0]
    tm = min(PROJ_TM, seq)
    pos_tiles = seq // tm
    return pl.pallas_call(
        _proj_kernel,
        grid=(n // tm, QKV_WIDTH // PROJ_TN),
        in_specs=[
            pl.BlockSpec((tm, D_MODEL), lambda i, j: (i, 0)),
            pl.BlockSpec((D_MODEL, PROJ_TN), lambda i, j: (0, j)),
            pl.BlockSpec((1, HEAD_DIM), lambda i, j: (0, 0)),
            pl.BlockSpec((1, HEAD_DIM), lambda i, j: (0, 0)),
            pl.BlockSpec((tm, HEAD_DIM), lambda i, j: (i % pos_tiles, 0)),
            pl.BlockSpec((tm, HEAD_DIM), lambda i, j: (i % pos_tiles, 0)),
        ],
        out_specs=pl.BlockSpec((tm, PROJ_TN), lambda i, j: (i, j)),
        out_shape=jax.ShapeDtypeStruct((n, QKV_WIDTH), BF16),
        compiler_params=_params(2),
        name="proj_qkv",
    )(u, w_qkv, gq, gk, cos2, sin2)


def _gate_kernel(u_ref, w_ref, b_ref, o_ref):
    y = jnp.dot(u_ref[...], w_ref[...], preferred_element_type=F32)
    o_ref[...] = jax.nn.sigmoid(y + b_ref[...])


def _gates(u, w_gate, b_gate, seq):
    n = u.shape[0]
    tm = min(PROJ_TM, seq)
    return pl.pallas_call(
        _gate_kernel,
        grid=(n // tm, GATE_WIDTH // PROJ_TN),
        in_specs=[
            pl.BlockSpec((tm, D_MODEL), lambda i, j: (i, 0)),
            pl.BlockSpec((D_MODEL, PROJ_TN), lambda i, j: (0, j)),
            pl.BlockSpec((1, PROJ_TN), lambda i, j: (0, j)),
        ],
        out_specs=pl.BlockSpec((tm, PROJ_TN), lambda i, j: (i, j)),
        out_shape=jax.ShapeDtypeStruct((n, GATE_WIDTH), F32),
        compiler_params=_params(2),
        name="proj_gates",
    )(u, w_gate, b_gate)


def _gqa_kernel(q_ref, k_ref, v_ref, o_ref, *, seq):
    tq = GQA_TQ
    q = jnp.concatenate([q_ref[0, :, g * HEAD_DIM:(g + 1) * HEAD_DIM] for g in range(GROUP_A)], axis=0)
    rows = GROUP_A * tq
    m = jnp.full((rows, 1), -jnp.inf, F32)
    l = jnp.zeros((rows, 1), F32)
    acc = jnp.zeros((rows, HEAD_DIM), F32)
    for c in range(seq // GQA_TK):
        k = k_ref[0, c * GQA_TK:(c + 1) * GQA_TK, :]
        v = v_ref[0, c * GQA_TK:(c + 1) * GQA_TK, :]
        s = lax.dot_general(q, k, (((1,), (1,)), ((), ())), preferred_element_type=F32)
        m_new = jnp.maximum(m, jnp.max(s, axis=-1, keepdims=True))
        alpha = jnp.exp(m - m_new)
        p = jnp.exp(s - m_new)
        l = alpha * l + jnp.sum(p, axis=-1, keepdims=True)
        acc = alpha * acc + jnp.dot(p.astype(BF16), v, preferred_element_type=F32)
        m = m_new
    o = acc / l
    for g in range(GROUP_A):
        o_ref[0, :, g * HEAD_DIM:(g + 1) * HEAD_DIM] = o[g * tq:(g + 1) * tq].astype(BF16)


def _gqa(qkv, batch, seq):
    gw = GROUP_A * HEAD_DIM
    return pl.pallas_call(
        functools.partial(_gqa_kernel, seq=seq),
        grid=(batch, N_KV_HEADS_A, seq // GQA_TQ),
        in_specs=[
            pl.BlockSpec((1, GQA_TQ, gw), lambda b, h, i: (b, i, h)),
            pl.BlockSpec((1, seq, HEAD_DIM), lambda b, h, i: (b, 0, CB_KA + h)),
            pl.BlockSpec((1, seq, HEAD_DIM), lambda b, h, i: (b, 0, CB_VA + h)),
        ],
        out_specs=pl.BlockSpec((1, GQA_TQ, gw), lambda b, h, i: (b, i, h)),
        out_shape=jax.ShapeDtypeStruct((batch, seq, WIDTH_QA), BF16),
        compiler_params=_params(3),
        name="gqa",
    )(qkv, qkv, qkv)


def _nbr_tables(rpb, rows):
    qc = np.arange(GRID_W)[:, None]
    kc = np.arange(GRID_W)[None, :]
    cs = np.clip(qc - WIN_COLS // 2, 0, GRID_W - WIN_COLS)
    col_ok = (kc >= cs) & (kc < cs + WIN_COLS)
    dc = np.clip(kc - qc + WIN_COLS - 1, 0, 2 * WIN_COLS - 2)
    per_dr = jnp.where(col_ok[None, None], rpb.astype(F32)[:, :, dc], NEG)
    blocked = jnp.full((N_HEADS_B, 1, GRID_W, GRID_W), NEG, F32)
    n_off = 2 * NBR_WIN - 1
    pad = (n_off - (2 * WIN_ROWS - 1)) // 2
    ext = jnp.concatenate([jnp.tile(blocked, (1, pad, 1, 1)), per_dr, jnp.tile(blocked, (1, pad, 1, 1))], axis=1)
    pair = jnp.concatenate([ext[:, :-1], ext[:, 1:]], axis=-1)

    n_steps = rows // NBR_ROWS
    masks = []
    for rb in (0, 1, n_steps - 1):
        r = rb * NBR_ROWS + np.arange(NBR_ROWS)
        ws = int(np.clip(rb * NBR_ROWS - WIN_ROWS // 2, 0, rows - NBR_WIN))
        rs = np.clip(r - WIN_ROWS // 2, 0, rows - WIN_ROWS)
        rk = ws + np.arange(NBR_WIN)
        ok = (rk[None, :] >= rs[:, None]) & (rk[None, :] < rs[:, None] + WIN_ROWS)
        m = np.where(ok, 0.0, NEG).astype(np.float32)
        masks.append(np.repeat(np.repeat(m, GRID_W, axis=0), GRID_W, axis=1))
    return pair, jnp.asarray(np.stack(masks))


def _nbr_kernel(q_ref, k_ref, v_ref, pair_ref, mask_ref, o_ref, *, rows):
    rb = pl.program_id(2)
    n_steps = rows // NBR_ROWS
    ws = jnp.clip(rb * NBR_ROWS - WIN_ROWS // 2, 0, rows - NBR_WIN)
    pattern = jnp.where(rb == 0, 0, jnp.where(rb == n_steps - 1, 2, 1))
    d0 = ws - rb * NBR_ROWS + (NBR_WIN - 1)
    start = pl.multiple_of(ws * GRID_W, GRID_W)
    k = k_ref[0, pl.ds(start, NBR_TK), :]
    v = v_ref[0, pl.ds(start, NBR_TK), :]
    s = lax.dot_general(q_ref[0], k, (((1,), (1,)), ((), ())), preferred_element_type=F32)
    bias = jnp.concatenate(
        [jnp.concatenate([pair_ref[0, d0 + 2 * c - rq] for c in range(NBR_WIN // 2)], axis=1)
         for rq in range(NBR_ROWS)], axis=0)
    s = s + bias + mask_ref[pattern]
    m = jnp.max(s, axis=-1, keepdims=True)
    p = jnp.exp(s - m)
    l = jnp.sum(p, axis=-1, keepdims=True)
    o = jnp.dot(p.astype(BF16), v, preferred_element_type=F32) / l
    o_ref[0] = o.astype(BF16)


def _nbr(qkv, pair, rowmask, batch, seq):
    rows = seq // GRID_W
    n_off = pair.shape[1]
    return pl.pallas_call(
        functools.partial(_nbr_kernel, rows=rows),
        grid=(batch, N_HEADS_B, rows // NBR_ROWS),
        in_specs=[
            pl.BlockSpec((1, NBR_TQ, HEAD_DIM), lambda b, h, i: (b, i, CB_QB + h)),
            pl.BlockSpec((1, seq, HEAD_DIM), lambda b, h, i: (b, 0, CB_KB + h)),
            pl.BlockSpec((1, seq, HEAD_DIM), lambda b, h, i: (b, 0, CB_VB + h)),
            pl.BlockSpec((1, n_off, GRID_W, 2 * GRID_W), lambda b, h, i: (h, 0, 0, 0)),
            pl.BlockSpec((3, NBR_TQ, NBR_TK), lambda b, h, i: (0, 0, 0)),
        ],
        out_specs=pl.BlockSpec((1, NBR_TQ, HEAD_DIM), lambda b, h, i: (b, i, h)),
        out_shape=jax.ShapeDtypeStruct((batch, seq, WIDTH_B), BF16),
        compiler_params=_params(3),
        name="nbr",
    )(qkv, qkv, qkv, pair, rowmask)


def _merge_kernel(h_ref, ya_ref, yb_ref, gate_ref, wa_ref, wb_ref, wo_ref, o_ref):
    ta = jnp.dot(ya_ref[...], wa_ref[...], preferred_element_type=F32)
    tb = jnp.dot(yb_ref[...], wb_ref[...], preferred_element_type=F32)
    merged = gate_ref[:, :D_MODEL] * ta + gate_ref[:, D_MODEL:] * tb
    o_ref[...] = h_ref[...] + jnp.dot(merged.astype(BF16), wo_ref[...], preferred_element_type=F32)


def _merge(h, ya, yb, gates, wa, wb, wo):
    n = h.shape[0]
    tm = MERGE_TM
    row = lambda i: (i, 0)
    fixed = lambda i: (0, 0)
    return pl.pallas_call(
        _merge_kernel,
        grid=(n // tm,),
        in_specs=[
            pl.BlockSpec((tm, D_MODEL), row),
            pl.BlockSpec((tm, WIDTH_QA), row),
            pl.BlockSpec((tm, WIDTH_B), row),
            pl.BlockSpec((tm, GATE_WIDTH), row),
            pl.BlockSpec((WIDTH_QA, D_MODEL), fixed),
            pl.BlockSpec((WIDTH_B, D_MODEL), fixed),
            pl.BlockSpec((D_MODEL, D_MODEL), fixed),
        ],
        out_specs=pl.BlockSpec((tm, D_MODEL), row),
        out_shape=jax.ShapeDtypeStruct((n, D_MODEL), F32),
        compiler_params=_params(1),
        name="merge_out",
    )(h, ya, yb, gates, wa, wb, wo)


def _rope_tables(seq):
    t = np.arange(seq)
    row = (t // GRID_W).astype(np.float32)
    col = (t % GRID_W).astype(np.float32)
    n_pairs_axis = HEAD_DIM // 4
    inv = (ROPE_THETA ** (-np.arange(n_pairs_axis, dtype=np.float32) / n_pairs_axis)).astype(np.float32)
    ang = np.concatenate([row[:, None] * inv[None], col[:, None] * inv[None]], axis=-1)
    cos = np.cos(ang).astype(np.float32)
    sin = np.sin(ang).astype(np.float32)
    cos2 = np.repeat(cos, 2, axis=-1)
    sin2 = np.stack([-sin, sin], axis=-1).reshape(seq, HEAD_DIM)
    return jnp.asarray(cos2), jnp.asarray(sin2)


def _ffn_weights(w_in, w_out):
    pad = D_FF_PAD - D_FF
    wg = jnp.pad(w_in[:, :D_FF].astype(BF16), ((0, 0), (0, pad)))
    wu = jnp.pad(w_in[:, D_FF:].astype(BF16), ((0, 0), (0, pad)))
    wo = jnp.pad(w_out.astype(BF16), ((0, pad), (0, 0)))
    return wg, wu, wo


def kernel(x_prompt, x_sample, g_ffn1, w_ffn1_in, w_ffn1_out, g_mix, w_in, b_gate, g_q_a, g_k_a, rpb_b,
           w_branch_a, w_branch_b, w_out, g_ffn2, w_ffn2_in, w_ffn2_out, g_final):
    assert g_ffn1.shape[0] == 1, "single-layer stack"
    w1 = _ffn_weights(w_ffn1_in[0], w_ffn1_out[0])
    w2 = _ffn_weights(w_ffn2_in[0], w_ffn2_out[0])
    w_qkv = w_in[0, :, :QKV_WIDTH].astype(BF16)
    w_gate = w_in[0, :, QKV_WIDTH:].astype(BF16)
    wa = w_branch_a[0].astype(BF16)
    wb = w_branch_b[0].astype(BF16)
    wo = w_out[0].astype(BF16)
    g1 = g_ffn1[0][None, :]
    gm = g_mix[0][None, :]
    g2 = g_ffn2[0][None, :]
    gf = g_final[None, :]
    gq = g_q_a[0][None, :]
    gk = g_k_a[0][None, :]
    bg = b_gate[0][None, :]

    def trunk(x):
        batch, seq, _ = x.shape
        n = batch * seq
        cos2, sin2 = _rope_tables(seq)
        pair, rowmask = _nbr_tables(rpb_b[0], seq // GRID_W)
        h, u = _ffn(x.reshape(n, D_MODEL), g1, *w1, gm, final=False)
        qkv = _proj(u, w_qkv, gq, gk, cos2, sin2, seq).reshape(batch, seq, QKV_WIDTH)
        gates = _gates(u, w_gate, bg, seq)
        ya = _gqa(qkv, batch, seq).reshape(n, WIDTH_QA)
        yb = _nbr(qkv, pair, rowmask, batch, seq).reshape(n, WIDTH_B)
        h2 = _merge(h, ya, yb, gates, wa, wb, wo)
        out = _ffn(h2, g2, *w2, gf, final=True)
        return out.reshape(batch, seq, D_MODEL)

    return (trunk(x_prompt), trunk(x_sample))
```

```python
import functools

import numpy as np
import jax
import jax.numpy as jnp
from jax import lax
from jax.experimental import pallas as pl
from jax.experimental.pallas import tpu as pltpu

D_MODEL = 2048
HEAD_DIM = 128
N_HEADS_A = 8
N_KV_HEADS_A = 2
GROUP_A = N_HEADS_A // N_KV_HEADS_A
N_HEADS_B = 8
GRID_W = 64
WIN_ROWS = 8
WIN_COLS = 16
D_FF = 5504
ROPE_THETA = 10000.0
EPS = 1e-6
NEG = -1e30

WIDTH_QA = N_HEADS_A * HEAD_DIM
WIDTH_KVA = N_KV_HEADS_A * HEAD_DIM
WIDTH_B = N_HEADS_B * HEAD_DIM
QKV_WIDTH = WIDTH_QA + 2 * WIDTH_KVA + 3 * WIDTH_B
GATE_WIDTH = 2 * D_MODEL

CB_QA = 0
CB_KA = WIDTH_QA // HEAD_DIM
CB_VA = CB_KA + N_KV_HEADS_A
CB_QB = CB_VA + N_KV_HEADS_A
CB_KB = CB_QB + N_HEADS_B
CB_VB = CB_KB + N_HEADS_B

V7X_VMEM_BYTES = 64 * 1024 * 1024
VMEM_LIMIT = V7X_VMEM_BYTES - 8 * 1024 * 1024

FF_TILE = 512
D_FF_PAD = -(-D_FF // FF_TILE) * FF_TILE
FFN_TM = 512
PROJ_TM = 2048
PROJ_TN = 512
PROJ_RC = 512
MERGE_TM = 512
MERGE_RC = 256
GQA_TQ = 256
GQA_STREAMS = 2
GQA_TK = 512
NBR_ROWS = 8
NBR_QR = 4
NBR_GROUPS = NBR_ROWS // NBR_QR
NBR_KR = 12
NBR_HEADS = 2
NBR_TQ = NBR_ROWS * GRID_W
NBR_GQ = NBR_QR * GRID_W
NBR_GK = NBR_KR * GRID_W
NBR_MAX_OFF = NBR_KR - 1

LOG2E = float(np.log2(np.e))
QK_SCALE = HEAD_DIM ** -0.5 * LOG2E

F32 = jnp.float32
BF16 = jnp.bfloat16


def _params(n_axes):
    return pltpu.CompilerParams(dimension_semantics=("arbitrary",) * n_axes,
                                vmem_limit_bytes=VMEM_LIMIT)


def _rms(xf, g):
    ms = jnp.mean(xf * xf, axis=-1, keepdims=True)
    return xf * lax.rsqrt(ms + EPS) * g


def _ffn_kernel(x_ref, g_ref, wg_ref, wu_ref, wo_ref, g2_ref, *refs, final):
    if final:
        o_ref, xn_ref, acc_ref = refs
    else:
        h_ref, u_ref, xn_ref, acc_ref = refs
    j = pl.program_id(1)

    @pl.when(j == 0)
    def _():
        xn_ref[...] = _rms(x_ref[...], g_ref[...]).astype(BF16)
        acc_ref[...] = jnp.zeros_like(acc_ref)

    xn = xn_ref[...]
    gate = jnp.dot(xn, wg_ref[...], preferred_element_type=F32)
    up = jnp.dot(xn, wu_ref[...], preferred_element_type=F32)
    act = ((gate * jax.nn.sigmoid(gate)) * up).astype(BF16)
    acc_ref[...] += jnp.dot(act, wo_ref[...], preferred_element_type=F32)

    @pl.when(j == pl.num_programs(1) - 1)
    def _():
        h = x_ref[...] + 0.5 * acc_ref[...]
        if final:
            o_ref[...] = _rms(h, g2_ref[...])
        else:
            h_ref[...] = h
            u_ref[...] = _rms(h, g2_ref[...]).astype(BF16)


def _ffn(x, g, wg, wu, wo, g2, *, final):
    n = x.shape[0]
    tm = FFN_TM
    nf = D_FF_PAD // FF_TILE
    row = lambda i, j: (i, 0)
    in_specs = [
        pl.BlockSpec((tm, D_MODEL), row),
        pl.BlockSpec((1, D_MODEL), lambda i, j: (0, 0)),
        pl.BlockSpec((D_MODEL, FF_TILE), lambda i, j: (0, j)),
        pl.BlockSpec((D_MODEL, FF_TILE), lambda i, j: (0, j)),
        pl.BlockSpec((FF_TILE, D_MODEL), lambda i, j: (j, 0)),
        pl.BlockSpec((1, D_MODEL), lambda i, j: (0, 0)),
    ]
    if final:
        out_shape = jax.ShapeDtypeStruct((n, D_MODEL), F32)
        out_specs = pl.BlockSpec((tm, D_MODEL), row)
    else:
        out_shape = (jax.ShapeDtypeStruct((n, D_MODEL), F32), jax.ShapeDtypeStruct((n, D_MODEL), BF16))
        out_specs = (pl.BlockSpec((tm, D_MODEL), row), pl.BlockSpec((tm, D_MODEL), row))
    return pl.pallas_call(
        functools.partial(_ffn_kernel, final=final),
        grid=(n // tm, nf),
        in_specs=in_specs,
        out_specs=out_specs,
        out_shape=out_shape,
        scratch_shapes=[pltpu.VMEM((tm, D_MODEL), BF16), pltpu.VMEM((tm, D_MODEL), F32)],
        compiler_params=_params(2),
        name="ffn_final" if final else "ffn_first",
    )(x, g, wg, wu, wo, g2)


def _rope(y, cos2, sin2):
    lane = lax.broadcasted_iota(jnp.int32, y.shape, 1)
    partner = jnp.where(lane % 2 == 0, pltpu.roll(y, HEAD_DIM - 1, 1), pltpu.roll(y, 1, 1))
    return y * cos2 + partner * sin2


def _row_chunks(n_rows):
    return [slice(r, r + PROJ_RC) for r in range(0, n_rows, PROJ_RC)]


def _proj_kernel(u_ref, w_ref, gq_ref, gk_ref, cos_ref, sin_ref, o_ref):
    j = pl.program_id(1)
    heads_per_tile = PROJ_TN // HEAD_DIM
    qa_tiles = WIDTH_QA // PROJ_TN
    kva_tile = qa_tiles
    qb_lo = (CB_QB * HEAD_DIM) // PROJ_TN
    qb_hi = (CB_KB * HEAD_DIM) // PROJ_TN

    def normed_rope(yh, g_ref, rows):
        return _rope(_rms(yh, g_ref[...]), cos_ref[rows, :], sin_ref[rows, :])

    def for_chunks(epilogue):
        chunks = _row_chunks(u_ref.shape[0])
        project = lambda rows: jnp.dot(u_ref[rows, :], w_ref[...], preferred_element_type=F32)
        y_next = project(chunks[0])
        for i, rows in enumerate(chunks):
            y = y_next
            if i + 1 < len(chunks):
                y_next = project(chunks[i + 1])
            epilogue(rows, y)

    @pl.when(j < qa_tiles)
    def _():
        def epilogue(rows, y):
            for hh in range(heads_per_tile):
                sl = slice(hh * HEAD_DIM, (hh + 1) * HEAD_DIM)
                o_ref[rows, sl] = (normed_rope(y[:, sl], gq_ref, rows) * QK_SCALE).astype(BF16)
        for_chunks(epilogue)

    @pl.when(j == kva_tile)
    def _():
        def epilogue(rows, y):
            for hh in range(N_KV_HEADS_A):
                sl = slice(hh * HEAD_DIM, (hh + 1) * HEAD_DIM)
                o_ref[rows, sl] = normed_rope(y[:, sl], gk_ref, rows).astype(BF16)
            o_ref[rows, WIDTH_KVA:] = y[:, WIDTH_KVA:].astype(BF16)
        for_chunks(epilogue)

    @pl.when((j >= qb_lo) & (j < qb_hi))
    def _():
        def epilogue(rows, y):
            o_ref[rows, :] = (y * QK_SCALE).astype(BF16)
        for_chunks(epilogue)

    @pl.when(j >= qb_hi)
    def _():
        def epilogue(rows, y):
            o_ref[rows, :] = y.astype(BF16)
        for_chunks(epilogue)


def _proj(u, w_qkv, gq, gk, cos2, sin2, seq):
    n = u.shape[0]
    tm = min(PROJ_TM, seq)
    pos_tiles = seq // tm
    return pl.pallas_call(
        _proj_kernel,
        grid=(n // tm, QKV_WIDTH // PROJ_TN),
        in_specs=[
            pl.BlockSpec((tm, D_MODEL), lambda i, j: (i, 0)),
            pl.BlockSpec((D_MODEL, PROJ_TN), lambda i, j: (0, j)),
            pl.BlockSpec((1, HEAD_DIM), lambda i, j: (0, 0)),
            pl.BlockSpec((1, HEAD_DIM), lambda i, j: (0, 0)),
            pl.BlockSpec((tm, HEAD_DIM), lambda i, j: (i % pos_tiles, 0)),
            pl.BlockSpec((tm, HEAD_DIM), lambda i, j: (i % pos_tiles, 0)),
        ],
        out_specs=pl.BlockSpec((tm, PROJ_TN), lambda i, j: (i, j)),
        out_shape=jax.ShapeDtypeStruct((n, QKV_WIDTH), BF16),
        compiler_params=_params(2),
        name="proj_qkv",
    )(u, w_qkv, gq, gk, cos2, sin2)


def _gate_kernel(u_ref, w_ref, b_ref, o_ref):
    chunks = _row_chunks(u_ref.shape[0])
    project = lambda rows: jnp.dot(u_ref[rows, :], w_ref[...], preferred_element_type=F32)
    y_next = project(chunks[0])
    for i, rows in enumerate(chunks):
        y = y_next
        if i + 1 < len(chunks):
            y_next = project(chunks[i + 1])
        o_ref[rows, :] = jax.nn.sigmoid(y + b_ref[...])


def _gates(u, w_gate, b_gate, seq):
    n = u.shape[0]
    tm = min(PROJ_TM, seq)
    return pl.pallas_call(
        _gate_kernel,
        grid=(n // tm, GATE_WIDTH // PROJ_TN),
        in_specs=[
            pl.BlockSpec((tm, D_MODEL), lambda i, j: (i, 0)),
            pl.BlockSpec((D_MODEL, PROJ_TN), lambda i, j: (0, j)),
            pl.BlockSpec((1, PROJ_TN), lambda i, j: (0, j)),
        ],
        out_specs=pl.BlockSpec((tm, PROJ_TN), lambda i, j: (i, j)),
        out_shape=jax.ShapeDtypeStruct((n, GATE_WIDTH), F32),
        compiler_params=_params(2),
        name="proj_gates",
    )(u, w_gate, b_gate)


def _gqa_kernel(q_ref, k_ref, v_ref, o_ref, *, seq):
    ts = GQA_TQ // GQA_STREAMS
    rows = GROUP_A * ts
    qs, ms, ls, accs = [], [], [], []
    for t in range(GQA_STREAMS):
        tok = slice(t * ts, (t + 1) * ts)
        qs.append(jnp.concatenate([q_ref[0, tok, g * HEAD_DIM:(g + 1) * HEAD_DIM] for g in range(GROUP_A)], axis=0))
        ms.append(jnp.full((1, rows), -jnp.inf, F32))
        ls.append(jnp.zeros((1, rows), F32))
        accs.append(jnp.zeros((HEAD_DIM, rows), F32))
    n_chunks = seq // GQA_TK

    def scores(c, t):
        k = k_ref[0, c * GQA_TK:(c + 1) * GQA_TK, :]
        return lax.dot_general(k, qs[t], (((1,), (1,)), ((), ())), preferred_element_type=F32)

    work = [(c, t) for c in range(n_chunks) for t in range(GQA_STREAMS)]
    st_next = scores(*work[0])
    for i, (c, t) in enumerate(work):
        st = st_next
        if i + 1 < len(work):
            st_next = scores(*work[i + 1])
        v = v_ref[0, c * GQA_TK:(c + 1) * GQA_TK, :]
        m_new = jnp.maximum(ms[t], jnp.max(st, axis=0, keepdims=True))
        alpha = jnp.exp2(ms[t] - m_new)
        p = jnp.exp2(st - m_new)
        ls[t] = alpha * ls[t] + jnp.sum(p, axis=0, keepdims=True)
        pv = lax.dot_general(v, p.astype(BF16), (((0,), (0,)), ((), ())), preferred_element_type=F32)
        accs[t] = alpha * accs[t] + pv
        ms[t] = m_new
    for t in range(GQA_STREAMS):
        o = (accs[t] / ls[t]).T
        for g in range(GROUP_A):
            o_ref[0, t * ts:(t + 1) * ts, g * HEAD_DIM:(g + 1) * HEAD_DIM] = o[g * ts:(g + 1) * ts].astype(BF16)


def _gqa(qkv, batch, seq):
    gw = GROUP_A * HEAD_DIM
    return pl.pallas_call(
        functools.partial(_gqa_kernel, seq=seq),
        grid=(batch, N_KV_HEADS_A, seq // GQA_TQ),
        in_specs=[
            pl.BlockSpec((1, GQA_TQ, gw), lambda b, h, i: (b, i, h)),
            pl.BlockSpec((1, seq, HEAD_DIM), lambda b, h, i: (b, 0, CB_KA + h)),
            pl.BlockSpec((1, seq, HEAD_DIM), lambda b, h, i: (b, 0, CB_VA + h)),
        ],
        out_specs=pl.BlockSpec((1, GQA_TQ, gw), lambda b, h, i: (b, i, h)),
        out_shape=jax.ShapeDtypeStruct((batch, seq, WIDTH_QA), BF16),
        compiler_params=_params(3),
        name="gqa",
    )(qkv, qkv, qkv)


def _nbr_tables(rpb, rows):
    qc = np.arange(GRID_W)[:, None]
    kc = np.arange(GRID_W)[None, :]
    cs = np.clip(qc - WIN_COLS // 2, 0, GRID_W - WIN_COLS)
    col_ok = (kc >= cs) & (kc < cs + WIN_COLS)
    dc = np.clip(kc - qc + WIN_COLS - 1, 0, 2 * WIN_COLS - 2)
    per_dr = jnp.where(col_ok[None, None], rpb.astype(F32)[:, :, dc] * LOG2E, NEG)
    pad = NBR_MAX_OFF - (WIN_ROWS - 1)
    blocked = jnp.full((N_HEADS_B, pad, GRID_W, GRID_W), NEG, F32)
    ext = jnp.concatenate([blocked, per_dr, blocked], axis=1)
    pair = jnp.concatenate([ext[:, :-1], ext[:, 1:]], axis=-1)

    n_groups = rows // NBR_QR
    assert n_groups >= 3
    masks = []
    for gg in (0, 1, n_groups - 1):
        r = gg * NBR_QR + np.arange(NBR_QR)
        ws = int(np.clip(gg * NBR_QR - WIN_ROWS // 2, 0, rows - NBR_KR))
        rs = np.clip(r - WIN_ROWS // 2, 0, rows - WIN_ROWS)
        rk = ws + np.arange(NBR_KR)
        ok = (rk[None, :] >= rs[:, None]) & (rk[None, :] < rs[:, None] + WIN_ROWS)
        assert ok.sum(axis=1).min() == WIN_ROWS
        m = np.where(ok, 0.0, NEG).astype(np.float32)
        masks.append(np.repeat(np.repeat(m, GRID_W, axis=0), GRID_W, axis=1))
    return pair, jnp.asarray(np.stack(masks))


def _nbr_kernel(q_ref, k_ref, v_ref, pair_ref, mask_ref, o_ref, *, rows):
    rb = pl.program_id(2)
    n_groups = rows // NBR_QR
    tiles = []
    for g in range(NBR_GROUPS):
        gg = rb * NBR_GROUPS + g
        ws = jnp.clip(gg * NBR_QR - WIN_ROWS // 2, 0, rows - NBR_KR)
        pattern = jnp.where(gg == 0, 0, jnp.where(gg == n_groups - 1, 2, 1))
        d0 = ws - gg * NBR_QR + NBR_MAX_OFF
        start = pl.multiple_of(ws * GRID_W, GRID_W)
        qrows = slice(g * NBR_GQ, (g + 1) * NBR_GQ)
        for hh in range(NBR_HEADS):
            tiles.append((hh, slice(hh * HEAD_DIM, (hh + 1) * HEAD_DIM), qrows, start, d0, pattern))

    def scores(tile):
        _, lanes, qrows, start, _, _ = tile
        k = k_ref[0, pl.ds(start, NBR_GK), lanes]
        return lax.dot_general(q_ref[0, qrows, lanes], k, (((1,), (1,)), ((), ())), preferred_element_type=F32)

    s_next = scores(tiles[0])
    for i, (hh, lanes, qrows, start, d0, pattern) in enumerate(tiles):
        s = s_next
        if i + 1 < len(tiles):
            s_next = scores(tiles[i + 1])
        bias = jnp.concatenate(
            [jnp.concatenate([pair_ref[hh, d0 + 2 * c - rq] for c in range(NBR_KR // 2)], axis=1)
             for rq in range(NBR_QR)], axis=0)
        s = s + bias + mask_ref[pattern]
        m = jnp.max(s, axis=-1, keepdims=True)
        p = jnp.exp2(s - m)
        l = jnp.sum(p, axis=-1, keepdims=True)
        v = v_ref[0, pl.ds(start, NBR_GK), lanes]
        o = jnp.dot(p.astype(BF16), v, preferred_element_type=F32) / l
        o_ref[0, qrows, lanes] = o.astype(BF16)


def _nbr(qkv, pair, rowmask, batch, seq):
    rows = seq // GRID_W
    n_off = pair.shape[1]
    hw = NBR_HEADS * HEAD_DIM
    cb_q, cb_k, cb_v = (cb // NBR_HEADS for cb in (CB_QB, CB_KB, CB_VB))
    return pl.pallas_call(
        functools.partial(_nbr_kernel, rows=rows),
        grid=(batch, N_HEADS_B // NBR_HEADS, rows // NBR_ROWS),
        in_specs=[
            pl.BlockSpec((1, NBR_TQ, hw), lambda b, h, i: (b, i, cb_q + h)),
            pl.BlockSpec((1, seq, hw), lambda b, h, i: (b, 0, cb_k + h)),
            pl.BlockSpec((1, seq, hw), lambda b, h, i: (b, 0, cb_v + h)),
            pl.BlockSpec((NBR_HEADS, n_off, GRID_W, 2 * GRID_W), lambda b, h, i: (h, 0, 0, 0)),
            pl.BlockSpec((3, NBR_GQ, NBR_GK), lambda b, h, i: (0, 0, 0)),
        ],
        out_specs=pl.BlockSpec((1, NBR_TQ, hw), lambda b, h, i: (b, i, h)),
        out_shape=jax.ShapeDtypeStruct((batch, seq, WIDTH_B), BF16),
        compiler_params=_params(3),
        name="nbr",
    )(qkv, qkv, qkv, pair, rowmask)


def _merge_kernel(h_ref, ya_ref, yb_ref, gate_ref, wa_ref, wb_ref, wo_ref, o_ref):
    for r in range(0, MERGE_TM, MERGE_RC):
        rows = slice(r, r + MERGE_RC)
        ta = jnp.dot(ya_ref[rows, :], wa_ref[...], preferred_element_type=F32)
        tb = jnp.dot(yb_ref[rows, :], wb_ref[...], preferred_element_type=F32)
        merged = gate_ref[rows, :D_MODEL] * ta + gate_ref[rows, D_MODEL:] * tb
        o_ref[rows, :] = h_ref[rows, :] + jnp.dot(merged.astype(BF16), wo_ref[...], preferred_element_type=F32)


def _merge(h, ya, yb, gates, wa, wb, wo):
    n = h.shape[0]
    tm = MERGE_TM
    row = lambda i: (i, 0)
    fixed = lambda i: (0, 0)
    return pl.pallas_call(
        _merge_kernel,
        grid=(n // tm,),
        in_specs=[
            pl.BlockSpec((tm, D_MODEL), row),
            pl.BlockSpec((tm, WIDTH_QA), row),
            pl.BlockSpec((tm, WIDTH_B), row),
            pl.BlockSpec((tm, GATE_WIDTH), row),
            pl.BlockSpec((WIDTH_QA, D_MODEL), fixed, pipeline_mode=pl.Buffered(1)),
            pl.BlockSpec((WIDTH_B, D_MODEL), fixed, pipeline_mode=pl.Buffered(1)),
            pl.BlockSpec((D_MODEL, D_MODEL), fixed, pipeline_mode=pl.Buffered(1)),
        ],
        out_specs=pl.BlockSpec((tm, D_MODEL), row),
        out_shape=jax.ShapeDtypeStruct((n, D_MODEL), F32),
        compiler_params=_params(1),
        name="merge_out",
    )(h, ya, yb, gates, wa, wb, wo)


def _rope_tables(seq):
    t = np.arange(seq)
    row = (t // GRID_W).astype(np.float32)
    col = (t % GRID_W).astype(np.float32)
    n_pairs_axis = HEAD_DIM // 4
    inv = (ROPE_THETA ** (-np.arange(n_pairs_axis, dtype=np.float32) / n_pairs_axis)).astype(np.float32)
    ang = np.concatenate([row[:, None] * inv[None], col[:, None] * inv[None]], axis=-1)
    cos = np.cos(ang).astype(np.float32)
    sin = np.sin(ang).astype(np.float32)
    cos2 = np.repeat(cos, 2, axis=-1)
    sin2 = np.stack([-sin, sin], axis=-1).reshape(seq, HEAD_DIM)
    return jnp.asarray(cos2), jnp.asarray(sin2)


def _ffn_weights(w_in, w_out):
    pad = D_FF_PAD - D_FF
    wg = jnp.pad(w_in[:, :D_FF].astype(BF16), ((0, 0), (0, pad)))
    wu = jnp.pad(w_in[:, D_FF:].astype(BF16), ((0, 0), (0, pad)))
    wo = jnp.pad(w_out.astype(BF16), ((0, pad), (0, 0)))
    return wg, wu, wo


def kernel(x_prompt, x_sample, g_ffn1, w_ffn1_in, w_ffn1_out, g_mix, w_in, b_gate, g_q_a, g_k_a, rpb_b,
           w_branch_a, w_branch_b, w_out, g_ffn2, w_ffn2_in, w_ffn2_out, g_final):
    assert g_ffn1.shape[0] == 1, "single-layer stack"
    w1 = _ffn_weights(w_ffn1_in[0], w_ffn1_out[0])
    w2 = _ffn_weights(w_ffn2_in[0], w_ffn2_out[0])
    w_qkv = w_in[0, :, :QKV_WIDTH].astype(BF16)
    w_gate = w_in[0, :, QKV_WIDTH:].astype(BF16)
    wa = w_branch_a[0].astype(BF16)
    wb = w_branch_b[0].astype(BF16)
    wo = w_out[0].astype(BF16)
    g1 = g_ffn1[0][None, :]
    gm = g_mix[0][None, :]
    g2 = g_ffn2[0][None, :]
    gf = g_final[None, :]
    gq = g_q_a[0][None, :]
    gk = g_k_a[0][None, :]
    bg = b_gate[0][None, :]

    def trunk(x):
        batch, seq, _ = x.shape
        n = batch * seq
        cos2, sin2 = _rope_tables(seq)
        pair, rowmask = _nbr_tables(rpb_b[0], seq // GRID_W)
        h, u = _ffn(x.reshape(n, D_MODEL), g1, *w1, gm, final=False)
        qkv = _proj(u, w_qkv, gq, gk, cos2, sin2, seq).reshape(batch, seq, QKV_WIDTH)
        gates = _gates(u, w_gate, bg, seq)
        ya = _gqa(qkv, batch, seq).reshape(n, WIDTH_QA)
        yb = _nbr(qkv, pair, rowmask, batch, seq).reshape(n, WIDTH_B)
        h2 = _merge(h, ya, yb, gates, wa, wb, wo)
        out = _ffn(h2, g2, *w2, gf, final=True)
        return out.reshape(batch, seq, D_MODEL)

    return (trunk(x_prompt), trunk(x_sample))
```

```python
import functools

import numpy as np
import jax
import jax.numpy as jnp
from jax import lax
from jax.experimental import pallas as pl
from jax.experimental.pallas import tpu as pltpu

D_MODEL = 2048
HEAD_DIM = 128
N_HEADS_A = 8
N_KV_HEADS_A = 2
GROUP_A = N_HEADS_A // N_KV_HEADS_A
N_HEADS_B = 8
GRID_W = 64
WIN_ROWS = 8
WIN_COLS = 16
D_FF = 5504
ROPE_THETA = 10000.0
EPS = 1e-6
NEG = -1e30

WIDTH_QA = N_HEADS_A * HEAD_DIM
WIDTH_KVA = N_KV_HEADS_A * HEAD_DIM
WIDTH_B = N_HEADS_B * HEAD_DIM
QKV_WIDTH = WIDTH_QA + 2 * WIDTH_KVA + 3 * WIDTH_B
GATE_WIDTH = 2 * D_MODEL

CB_QA = 0
CB_KA = WIDTH_QA // HEAD_DIM
CB_VA = CB_KA + N_KV_HEADS_A
CB_QB = CB_VA + N_KV_HEADS_A
CB_KB = CB_QB + N_HEADS_B
CB_VB = CB_KB + N_HEADS_B

V7X_VMEM_BYTES = 64 * 1024 * 1024
VMEM_LIMIT = V7X_VMEM_BYTES - 8 * 1024 * 1024

FF_TILE = 512
D_FF_PAD = -(-D_FF // FF_TILE) * FF_TILE
FFN_TM = 512
FFN_NORM_ROWS = 64
PROJ_TM = 2048
PROJ_TN = 512
PROJ_RC = 512
MERGE_TM = 512
MERGE_RC = 256
GQA_TQ = 256
GQA_STREAMS = 2
GQA_TK = 512
NBR_ROWS = 8
NBR_QR = 4
NBR_GROUPS = NBR_ROWS // NBR_QR
NBR_KR = 12
NBR_HEADS = 2
NBR_TQ = NBR_ROWS * GRID_W
NBR_GQ = NBR_QR * GRID_W
NBR_GK = NBR_KR * GRID_W
NBR_MAX_OFF = NBR_KR - 1

LOG2E = float(np.log2(np.e))
QK_SCALE = HEAD_DIM ** -0.5 * LOG2E

F32 = jnp.float32
BF16 = jnp.bfloat16


def _params(n_axes):
    return pltpu.CompilerParams(dimension_semantics=("arbitrary",) * n_axes,
                                vmem_limit_bytes=VMEM_LIMIT)


def _rms(xf, g):
    ms = jnp.mean(xf * xf, axis=-1, keepdims=True)
    return xf * lax.rsqrt(ms + EPS) * g


def _ffn_kernel(x_ref, xnext_ref, g_ref, wg_ref, wu_ref, wo_ref, g2_ref, *refs, final):
    if final:
        o_ref, xn_even_ref, xn_odd_ref, acc_ref = refs
    else:
        h_ref, u_ref, xn_even_ref, xn_odd_ref, acc_ref = refs
    i = pl.program_id(0)
    j = pl.program_id(1)
    slabs = FFN_TM // FFN_NORM_ROWS

    @pl.when((i == 0) & (j == 0))
    def _():
        xn_even_ref[...] = _rms(x_ref[...], g_ref[...]).astype(BF16)
        acc_ref[...] = jnp.zeros_like(acc_ref)

    def step(xn_ref, xn_next_ref):
        xn = xn_ref[...]
        gate = jnp.dot(xn, wg_ref[...], preferred_element_type=F32)
        up = jnp.dot(xn, wu_ref[...], preferred_element_type=F32)
        act = ((gate * jax.nn.sigmoid(gate)) * up).astype(BF16)
        prev = jnp.where(j == 0, 0.0, acc_ref[...])
        acc_ref[...] = prev + jnp.dot(act, wo_ref[...], preferred_element_type=F32)
        r0 = pl.multiple_of(jnp.minimum(j, slabs - 1) * FFN_NORM_ROWS, FFN_NORM_ROWS)
        slab = pl.ds(r0, FFN_NORM_ROWS)
        xn_next_ref[slab, :] = _rms(xnext_ref[slab, :], g_ref[...]).astype(BF16)

    @pl.when(i % 2 == 0)
    def _():
        step(xn_even_ref, xn_odd_ref)

    @pl.when(i % 2 == 1)
    def _():
        step(xn_odd_ref, xn_even_ref)

    @pl.when(j == pl.num_programs(1) - 1)
    def _():
        h = x_ref[...] + 0.5 * acc_ref[...]
        if final:
            o_ref[...] = _rms(h, g2_ref[...])
        else:
            h_ref[...] = h
            u_ref[...] = _rms(h, g2_ref[...]).astype(BF16)


def _ffn(x, g, wg, wu, wo, g2, *, final):
    n = x.shape[0]
    tm = FFN_TM
    nf = D_FF_PAD // FF_TILE
    n_tiles = n // tm
    assert nf >= tm // FFN_NORM_ROWS
    row = lambda i, j: (i, 0)
    in_specs = [
        pl.BlockSpec((tm, D_MODEL), row),
        pl.BlockSpec((tm, D_MODEL), lambda i, j: (jnp.minimum(i + 1, n_tiles - 1), 0)),
        pl.BlockSpec((1, D_MODEL), lambda i, j: (0, 0)),
        pl.BlockSpec((D_MODEL, FF_TILE), lambda i, j: (0, j)),
        pl.BlockSpec((D_MODEL, FF_TILE), lambda i, j: (0, j)),
        pl.BlockSpec((FF_TILE, D_MODEL), lambda i, j: (j, 0)),
        pl.BlockSpec((1, D_MODEL), lambda i, j: (0, 0)),
    ]
    if final:
        out_shape = jax.ShapeDtypeStruct((n, D_MODEL), F32)
        out_specs = pl.BlockSpec((tm, D_MODEL), row)
    else:
        out_shape = (jax.ShapeDtypeStruct((n, D_MODEL), F32), jax.ShapeDtypeStruct((n, D_MODEL), BF16))
        out_specs = (pl.BlockSpec((tm, D_MODEL), row), pl.BlockSpec((tm, D_MODEL), row))
    return pl.pallas_call(
        functools.partial(_ffn_kernel, final=final),
        grid=(n_tiles, nf),
        in_specs=in_specs,
        out_specs=out_specs,
        out_shape=out_shape,
        scratch_shapes=[pltpu.VMEM((tm, D_MODEL), BF16), pltpu.VMEM((tm, D_MODEL), BF16),
                        pltpu.VMEM((tm, D_MODEL), F32)],
        compiler_params=_params(2),
        name="ffn_final" if final else "ffn_first",
    )(x, x, g, wg, wu, wo, g2)


def _rope(y, cos2, sin2):
    lane = lax.broadcasted_iota(jnp.int32, y.shape, 1)
    partner = jnp.where(lane % 2 == 0, pltpu.roll(y, HEAD_DIM - 1, 1), pltpu.roll(y, 1, 1))
    return y * cos2 + partner * sin2


def _row_chunks(n_rows):
    return [slice(r, r + PROJ_RC) for r in range(0, n_rows, PROJ_RC)]


def _proj_kernel(u_ref, w_ref, gq_ref, gk_ref, cos_ref, sin_ref, o_ref):
    j = pl.program_id(1)
    heads_per_tile = PROJ_TN // HEAD_DIM
    qa_tiles = WIDTH_QA // PROJ_TN
    kva_tile = qa_tiles
    qb_lo = (CB_QB * HEAD_DIM) // PROJ_TN
    qb_hi = (CB_KB * HEAD_DIM) // PROJ_TN

    def normed_rope(yh, g_ref, rows):
        return _rope(_rms(yh, g_ref[...]), cos_ref[rows, :], sin_ref[rows, :])

    def for_chunks(epilogue):
        w = w_ref[...].astype(BF16)
        for rows in _row_chunks(u_ref.shape[0]):
            epilogue(rows, jnp.dot(u_ref[rows, :], w, preferred_element_type=F32))

    @pl.when(j < qa_tiles)
    def _():
        def epilogue(rows, y):
            for hh in range(heads_per_tile):
                sl = slice(hh * HEAD_DIM, (hh + 1) * HEAD_DIM)
                o_ref[rows, sl] = (normed_rope(y[:, sl], gq_ref, rows) * QK_SCALE).astype(BF16)
        for_chunks(epilogue)

    @pl.when(j == kva_tile)
    def _():
        def epilogue(rows, y):
            for hh in range(N_KV_HEADS_A):
                sl = slice(hh * HEAD_DIM, (hh + 1) * HEAD_DIM)
                o_ref[rows, sl] = normed_rope(y[:, sl], gk_ref, rows).astype(BF16)
            o_ref[rows, WIDTH_KVA:] = y[:, WIDTH_KVA:].astype(BF16)
        for_chunks(epilogue)

    @pl.when((j >= qb_lo) & (j < qb_hi))
    def _():
        def epilogue(rows, y):
            o_ref[rows, :] = (y * QK_SCALE).astype(BF16)
        for_chunks(epilogue)

    @pl.when(j >= qb_hi)
    def _():
        def epilogue(rows, y):
            o_ref[rows, :] = y.astype(BF16)
        for_chunks(epilogue)


def _proj(u, w_in, gq, gk, cos2, sin2, seq):
    n = u.shape[0]
    tm = min(PROJ_TM, seq)
    pos_tiles = seq // tm
    return pl.pallas_call(
        _proj_kernel,
        grid=(n // tm, QKV_WIDTH // PROJ_TN),
        in_specs=[
            pl.BlockSpec((tm, D_MODEL), lambda i, j: (i, 0)),
            pl.BlockSpec((D_MODEL, PROJ_TN), lambda i, j: (0, j)),
            pl.BlockSpec((1, HEAD_DIM), lambda i, j: (0, 0)),
            pl.BlockSpec((1, HEAD_DIM), lambda i, j: (0, 0)),
            pl.BlockSpec((tm, HEAD_DIM), lambda i, j: (i % pos_tiles, 0)),
            pl.BlockSpec((tm, HEAD_DIM), lambda i, j: (i % pos_tiles, 0)),
        ],
        out_specs=pl.BlockSpec((tm, PROJ_TN), lambda i, j: (i, j)),
        out_shape=jax.ShapeDtypeStruct((n, QKV_WIDTH), BF16),
        compiler_params=_params(2),
        name="proj_qkv",
    )(u, w_in, gq, gk, cos2, sin2)


def _gate_kernel(u_ref, w_ref, b_ref, o_ref):
    w = w_ref[...].astype(BF16)
    for rows in _row_chunks(u_ref.shape[0]):
        y = jnp.dot(u_ref[rows, :], w, preferred_element_type=F32)
        o_ref[rows, :] = jax.nn.sigmoid(y + b_ref[...])


def _gates(u, w_in, b_gate, seq):
    n = u.shape[0]
    tm = min(PROJ_TM, seq)
    first = QKV_WIDTH // PROJ_TN
    return pl.pallas_call(
        _gate_kernel,
        grid=(n // tm, GATE_WIDTH // PROJ_TN),
        in_specs=[
            pl.BlockSpec((tm, D_MODEL), lambda i, j: (i, 0)),
            pl.BlockSpec((D_MODEL, PROJ_TN), lambda i, j: (0, first + j)),
            pl.BlockSpec((1, PROJ_TN), lambda i, j: (0, j)),
        ],
        out_specs=pl.BlockSpec((tm, PROJ_TN), lambda i, j: (i, j)),
        out_shape=jax.ShapeDtypeStruct((n, GATE_WIDTH), F32),
        compiler_params=_params(2),
        name="proj_gates",
    )(u, w_in, b_gate)


def _gqa_kernel(q_ref, k_ref, v_ref, o_ref, *, seq):
    ts = GQA_TQ // GQA_STREAMS
    rows = GROUP_A * ts
    qs, ms, ls, accs = [], [], [], []
    for t in range(GQA_STREAMS):
        tok = slice(t * ts, (t + 1) * ts)
        qs.append(jnp.concatenate([q_ref[0, tok, g * HEAD_DIM:(g + 1) * HEAD_DIM] for g in range(GROUP_A)], axis=0))
        ms.append(jnp.full((1, rows), -jnp.inf, F32))
        ls.append(jnp.zeros((1, rows), F32))
        accs.append(jnp.zeros((HEAD_DIM, rows), F32))
    n_chunks = seq // GQA_TK

    def scores(c, t):
        k = k_ref[0, c * GQA_TK:(c + 1) * GQA_TK, :]
        return lax.dot_general(k, qs[t], (((1,), (1,)), ((), ())), preferred_element_type=F32)

    work = [(c, t) for c in range(n_chunks) for t in range(GQA_STREAMS)]
    st_next = scores(*work[0])
    for i, (c, t) in enumerate(work):
        st = st_next
        if i + 1 < len(work):
            st_next = scores(*work[i + 1])
        v = v_ref[0, c * GQA_TK:(c + 1) * GQA_TK, :]
        m_new = jnp.maximum(ms[t], jnp.max(st, axis=0, keepdims=True))
        alpha = jnp.exp2(ms[t] - m_new)
        p = jnp.exp2(st - m_new)
        ls[t] = alpha * ls[t] + jnp.sum(p, axis=0, keepdims=True)
        pv = lax.dot_general(v, p.astype(BF16), (((0,), (0,)), ((), ())), preferred_element_type=F32)
        accs[t] = alpha * accs[t] + pv
        ms[t] = m_new
    for t in range(GQA_STREAMS):
        o = (accs[t] / ls[t]).T
        for g in range(GROUP_A):
            o_ref[0, t * ts:(t + 1) * ts, g * HEAD_DIM:(g + 1) * HEAD_DIM] = o[g * ts:(g + 1) * ts].astype(BF16)


def _gqa(qkv, batch, seq):
    gw = GROUP_A * HEAD_DIM
    return pl.pallas_call(
        functools.partial(_gqa_kernel, seq=seq),
        grid=(batch, N_KV_HEADS_A, seq // GQA_TQ),
        in_specs=[
            pl.BlockSpec((1, GQA_TQ, gw), lambda b, h, i: (b, i, h)),
            pl.BlockSpec((1, seq, HEAD_DIM), lambda b, h, i: (b, 0, CB_KA + h)),
            pl.BlockSpec((1, seq, HEAD_DIM), lambda b, h, i: (b, 0, CB_VA + h)),
        ],
        out_specs=pl.BlockSpec((1, GQA_TQ, gw), lambda b, h, i: (b, i, h)),
        out_shape=jax.ShapeDtypeStruct((batch, seq, WIDTH_QA), BF16),
        compiler_params=_params(3),
        name="gqa",
    )(qkv, qkv, qkv)


def _nbr_tables(rpb, rows):
    qc = np.arange(GRID_W)[:, None]
    kc = np.arange(GRID_W)[None, :]
    cs = np.clip(qc - WIN_COLS // 2, 0, GRID_W - WIN_COLS)
    col_ok = (kc >= cs) & (kc < cs + WIN_COLS)
    dc = np.clip(kc - qc + WIN_COLS - 1, 0, 2 * WIN_COLS - 2)
    onehot = (dc.reshape(1, -1) == np.arange(2 * WIN_COLS - 1)[:, None]).astype(np.float32)
    n_dr = 2 * WIN_ROWS - 1
    picked = jnp.dot(rpb.astype(F32).reshape(N_HEADS_B * n_dr, -1), jnp.asarray(onehot),
                     precision=lax.Precision.HIGHEST).reshape(N_HEADS_B, n_dr, GRID_W, GRID_W)
    per_dr = jnp.where(col_ok[None, None], picked * LOG2E, NEG)
    pad = NBR_MAX_OFF - (WIN_ROWS - 1)
    blocked = jnp.full((N_HEADS_B, pad, GRID_W, GRID_W), NEG, F32)
    ext = jnp.concatenate([blocked, per_dr, blocked], axis=1)
    pair = jnp.concatenate([ext[:, :-1], ext[:, 1:]], axis=-1)

    n_groups = rows // NBR_QR
    assert n_groups >= 3
    masks = []
    for gg in (0, 1, n_groups - 1):
        r = gg * NBR_QR + np.arange(NBR_QR)
        ws = int(np.clip(gg * NBR_QR - WIN_ROWS // 2, 0, rows - NBR_KR))
        rs = np.clip(r - WIN_ROWS // 2, 0, rows - WIN_ROWS)
        rk = ws + np.arange(NBR_KR)
        ok = (rk[None, :] >= rs[:, None]) & (rk[None, :] < rs[:, None] + WIN_ROWS)
        assert ok.sum(axis=1).min() == WIN_ROWS
        m = np.where(ok, 0.0, NEG).astype(np.float32)
        masks.append(np.repeat(np.repeat(m, GRID_W, axis=0), GRID_W, axis=1))
    return pair, jnp.asarray(np.stack(masks))


def _nbr_kernel(q_ref, *refs, rows):
    k_refs = refs[:NBR_HEADS]
    v_refs = refs[NBR_HEADS:2 * NBR_HEADS]
    pair_ref, mask_ref, o_ref = refs[2 * NBR_HEADS:]
    rb = pl.program_id(2)
    n_groups = rows // NBR_QR
    tiles = []
    for g in range(NBR_GROUPS):
        gg = rb * NBR_GROUPS + g
        ws = jnp.clip(gg * NBR_QR - WIN_ROWS // 2, 0, rows - NBR_KR)
        pattern = jnp.where(gg == 0, 0, jnp.where(gg == n_groups - 1, 2, 1))
        d0 = ws - gg * NBR_QR + NBR_MAX_OFF
        start = pl.multiple_of(ws * GRID_W, GRID_W)
        qrows = slice(g * NBR_GQ, (g + 1) * NBR_GQ)
        for hh in range(NBR_HEADS):
            tiles.append((hh, slice(hh * HEAD_DIM, (hh + 1) * HEAD_DIM), qrows, start, d0, pattern))

    def scores(tile):
        hh, lanes, qrows, start, _, _ = tile
        k = k_refs[hh][0, pl.ds(start, NBR_GK), :]
        return lax.dot_general(q_ref[0, qrows, lanes], k, (((1,), (1,)), ((), ())), preferred_element_type=F32)

    s_next = scores(tiles[0])
    for i, (hh, lanes, qrows, start, d0, pattern) in enumerate(tiles):
        s = s_next
        if i + 1 < len(tiles):
            s_next = scores(tiles[i + 1])
        bias = jnp.concatenate(
            [jnp.concatenate([pair_ref[hh, d0 + 2 * c - rq] for c in range(NBR_KR // 2)], axis=1)
             for rq in range(NBR_QR)], axis=0)
        s = s + bias + mask_ref[pattern]
        m = jnp.max(s, axis=-1, keepdims=True)
        p = jnp.exp2(s - m)
        l = jnp.sum(p, axis=-1, keepdims=True)
        v = v_refs[hh][0, pl.ds(start, NBR_GK), :]
        o = jnp.dot(p.astype(BF16), v, preferred_element_type=F32) / l
        o_ref[0, qrows, lanes] = o.astype(BF16)


def _nbr(qkv, pair, rowmask, batch, seq):
    rows = seq // GRID_W
    n_off = pair.shape[1]
    hw = NBR_HEADS * HEAD_DIM
    cb_q = CB_QB // NBR_HEADS

    def head_block(cb, hh):
        return pl.BlockSpec((1, seq, HEAD_DIM), lambda b, h, i: (b, 0, cb + NBR_HEADS * h + hh))

    return pl.pallas_call(
        functools.partial(_nbr_kernel, rows=rows),
        grid=(batch, N_HEADS_B // NBR_HEADS, rows // NBR_ROWS),
        in_specs=[pl.BlockSpec((1, NBR_TQ, hw), lambda b, h, i: (b, i, cb_q + h))]
        + [head_block(CB_KB, hh) for hh in range(NBR_HEADS)]
        + [head_block(CB_VB, hh) for hh in range(NBR_HEADS)]
        + [pl.BlockSpec((NBR_HEADS, n_off, GRID_W, 2 * GRID_W), lambda b, h, i: (h, 0, 0, 0)),
           pl.BlockSpec((3, NBR_GQ, NBR_GK), lambda b, h, i: (0, 0, 0))],
        out_specs=pl.BlockSpec((1, NBR_TQ, hw), lambda b, h, i: (b, i, h)),
        out_shape=jax.ShapeDtypeStruct((batch, seq, WIDTH_B), BF16),
        compiler_params=_params(3),
        name="nbr",
    )(qkv, *([qkv] * (2 * NBR_HEADS)), pair, rowmask)


def _merge_kernel(h_ref, ya_ref, yb_ref, gate_ref, wa_ref, wb_ref, wo_ref, o_ref):
    for r in range(0, MERGE_TM, MERGE_RC):
        rows = slice(r, r + MERGE_RC)
        ta = jnp.dot(ya_ref[rows, :], wa_ref[...], preferred_element_type=F32)
        tb = jnp.dot(yb_ref[rows, :], wb_ref[...], preferred_element_type=F32)
        merged = gate_ref[rows, :D_MODEL] * ta + gate_ref[rows, D_MODEL:] * tb
        o_ref[rows, :] = h_ref[rows, :] + jnp.dot(merged.astype(BF16), wo_ref[...], preferred_element_type=F32)


def _merge(h, ya, yb, gates, wa, wb, wo):
    n = h.shape[0]
    tm = MERGE_TM
    row = lambda i: (i, 0)
    fixed = lambda i: (0, 0)
    return pl.pallas_call(
        _merge_kernel,
        grid=(n // tm,),
        in_specs=[
            pl.BlockSpec((tm, D_MODEL), row),
            pl.BlockSpec((tm, WIDTH_QA), row),
            pl.BlockSpec((tm, WIDTH_B), row),
            pl.BlockSpec((tm, GATE_WIDTH), row),
            pl.BlockSpec((WIDTH_QA, D_MODEL), fixed, pipeline_mode=pl.Buffered(1)),
            pl.BlockSpec((WIDTH_B, D_MODEL), fixed, pipeline_mode=pl.Buffered(1)),
            pl.BlockSpec((D_MODEL, D_MODEL), fixed, pipeline_mode=pl.Buffered(1)),
        ],
        out_specs=pl.BlockSpec((tm, D_MODEL), row),
        out_shape=jax.ShapeDtypeStruct((n, D_MODEL), F32),
        compiler_params=_params(1),
        name="merge_out",
    )(h, ya, yb, gates, wa, wb, wo)


def _rope_tables(seq):
    t = np.arange(seq)
    row = (t // GRID_W).astype(np.float32)
    col = (t % GRID_W).astype(np.float32)
    n_pairs_axis = HEAD_DIM // 4
    inv = (ROPE_THETA ** (-np.arange(n_pairs_axis, dtype=np.float32) / n_pairs_axis)).astype(np.float32)
    ang = np.concatenate([row[:, None] * inv[None], col[:, None] * inv[None]], axis=-1)
    cos = np.cos(ang).astype(np.float32)
    sin = np.sin(ang).astype(np.float32)
    cos2 = np.repeat(cos, 2, axis=-1)
    sin2 = np.stack([-sin, sin], axis=-1).reshape(seq, HEAD_DIM)
    return jnp.asarray(cos2), jnp.asarray(sin2)


def _ffn_weights(w_in, w_out):
    pad = D_FF_PAD - D_FF
    wg = jnp.pad(w_in[:, :D_FF].astype(BF16), ((0, 0), (0, pad)))
    wu = jnp.pad(w_in[:, D_FF:].astype(BF16), ((0, 0), (0, pad)))
    wo = jnp.pad(w_out.astype(BF16), ((0, pad), (0, 0)))
    return wg, wu, wo


def kernel(x_prompt, x_sample, g_ffn1, w_ffn1_in, w_ffn1_out, g_mix, w_in, b_gate, g_q_a, g_k_a, rpb_b,
           w_branch_a, w_branch_b, w_out, g_ffn2, w_ffn2_in, w_ffn2_out, g_final):
    assert g_ffn1.shape[0] == 1, "single-layer stack"
    w1 = _ffn_weights(w_ffn1_in[0], w_ffn1_out[0])
    w2 = _ffn_weights(w_ffn2_in[0], w_ffn2_out[0])
    w_proj = w_in[0]
    wa = w_branch_a[0].astype(BF16)
    wb = w_branch_b[0].astype(BF16)
    wo = w_out[0].astype(BF16)
    g1 = g_ffn1[0][None, :]
    gm = g_mix[0][None, :]
    g2 = g_ffn2[0][None, :]
    gf = g_final[None, :]
    gq = g_q_a[0][None, :]
    gk = g_k_a[0][None, :]
    bg = b_gate[0][None, :]

    def trunk(x):
        batch, seq, _ = x.shape
        n = batch * seq
        cos2, sin2 = _rope_tables(seq)
        pair, rowmask = _nbr_tables(rpb_b[0], seq // GRID_W)
        h, u = _ffn(x.reshape(n, D_MODEL), g1, *w1, gm, final=False)
        qkv = _proj(u, w_proj, gq, gk, cos2, sin2, seq).reshape(batch, seq, QKV_WIDTH)
        gates = _gates(u, w_proj, bg, seq)
        ya = _gqa(qkv, batch, seq).reshape(n, WIDTH_QA)
        yb = _nbr(qkv, pair, rowmask, batch, seq).reshape(n, WIDTH_B)
        h2 = _merge(h, ya, yb, gates, wa, wb, wo)
        out = _ffn(h2, g2, *w2, gf, final=True)
        return out.reshape(batch, seq, D_MODEL)

    return (trunk(x_prompt), trunk(x_sample))
```

```python
import functools

import numpy as np
import jax
import jax.numpy as jnp
from jax import lax
from jax.experimental import pallas as pl
from jax.experimental.pallas import tpu as pltpu

D_MODEL = 2048
HEAD_DIM = 128
N_HEADS_A = 8
N_KV_HEADS_A = 2
GROUP_A = N_HEADS_A // N_KV_HEADS_A
N_HEADS_B = 8
GRID_W = 64
WIN_ROWS = 8
WIN_COLS = 16
D_FF = 5504
ROPE_THETA = 10000.0
EPS = 1e-6
NEG = -1e30

WIDTH_QA = N_HEADS_A * HEAD_DIM
WIDTH_KVA = N_KV_HEADS_A * HEAD_DIM
WIDTH_B = N_HEADS_B * HEAD_DIM
QKV_WIDTH = WIDTH_QA + 2 * WIDTH_KVA + 3 * WIDTH_B
GATE_WIDTH = 2 * D_MODEL
IN_WIDTH = QKV_WIDTH + GATE_WIDTH
N_QKV_HEADS = QKV_WIDTH // HEAD_DIM

CB_QA = 0
CB_KA = WIDTH_QA // HEAD_DIM
CB_VA = CB_KA + N_KV_HEADS_A
CB_QB = CB_VA + N_KV_HEADS_A
CB_KB = CB_QB + N_HEADS_B
CB_VB = CB_KB + N_HEADS_B

V7X_VMEM_BYTES = 64 * 1024 * 1024
VMEM_LIMIT = V7X_VMEM_BYTES - 8 * 1024 * 1024

FF_TILE = 512
D_FF_PAD = -(-D_FF // FF_TILE) * FF_TILE
FFN_TM = 512
CAST_ROWS = 128
PROJ_TM = 2048
PROJ_TN = 512
PROJ_RC = 512
MERGE_TM = 512
MERGE_RC = 256
GQA_TQ = 256
GQA_STREAMS = 2
GQA_TK = 512
NBR_ROWS = 8
NBR_QR = 4
NBR_GROUPS = NBR_ROWS // NBR_QR
NBR_KR = 12
NBR_HEADS = 2
NBR_TQ = NBR_ROWS * GRID_W
NBR_GQ = NBR_QR * GRID_W
NBR_GK = NBR_KR * GRID_W
NBR_MAX_OFF = NBR_KR - 1

LOG2E = float(np.log2(np.e))
QK_SCALE = HEAD_DIM ** -0.5 * LOG2E

F32 = jnp.float32
BF16 = jnp.bfloat16


def _params(n_axes):
    return pltpu.CompilerParams(dimension_semantics=("arbitrary",) * n_axes,
                                vmem_limit_bytes=VMEM_LIMIT)


def _rms(xf, g):
    ms = jnp.mean(xf * xf, axis=-1, keepdims=True)
    return xf * lax.rsqrt(ms + EPS) * g


def _cast_cols_kernel(w_ref, o_ref, *, valid):
    n_chunks, _, tn = o_ref.shape
    for c in range(n_chunks):
        lo = c * tn
        width = min(tn, valid - lo)
        o_ref[c, :, :width] = w_ref[:, lo:lo + width].astype(BF16)
        if width < tn:
            o_ref[c, :, width:] = jnp.zeros((o_ref.shape[1], tn - width), BF16)


def _cast_cols(w, n_parts, tn):
    r, total = w.shape
    valid = total // n_parts
    n_chunks = -(-valid // tn)
    return pl.pallas_call(
        functools.partial(_cast_cols_kernel, valid=valid),
        grid=(n_parts, r // CAST_ROWS),
        in_specs=[pl.BlockSpec((CAST_ROWS, valid), lambda p, i: (i, p))],
        out_specs=pl.BlockSpec((None, n_chunks, CAST_ROWS, tn), lambda p, i: (p, 0, i, 0)),
        out_shape=jax.ShapeDtypeStruct((n_parts, n_chunks, r, tn), BF16),
        compiler_params=_params(2),
        name="cast_cols",
    )(w)


def _cast_rows_kernel(w_ref, o_ref, *, valid_tiles):
    i = pl.program_id(0)
    o_ref[...] = jnp.where(i < valid_tiles, w_ref[...], 0.0).astype(BF16)


def _cast_rows(w, rows_out):
    r, c = w.shape
    valid_tiles = r // CAST_ROWS
    assert valid_tiles * CAST_ROWS == r
    return pl.pallas_call(
        functools.partial(_cast_rows_kernel, valid_tiles=valid_tiles),
        grid=(rows_out // CAST_ROWS,),
        in_specs=[pl.BlockSpec((CAST_ROWS, c), lambda i: (jnp.minimum(i, valid_tiles - 1), 0))],
        out_specs=pl.BlockSpec((CAST_ROWS, c), lambda i: (i, 0)),
        out_shape=jax.ShapeDtypeStruct((rows_out, c), BF16),
        compiler_params=_params(1),
        name="cast_rows",
    )(w)


def _ffn_kernel(x_ref, g_ref, wg_ref, wu_ref, wo_ref, g2_ref, *refs, final):
    if final:
        o_ref, xn_ref, acc_ref = refs
    else:
        h_ref, u_ref, xn_ref, acc_ref = refs
    j = pl.program_id(1)

    @pl.when(j == 0)
    def _():
        xn_ref[...] = _rms(x_ref[...], g_ref[...]).astype(BF16)
        acc_ref[...] = jnp.zeros_like(acc_ref)

    xn = xn_ref[...]
    gate = jnp.dot(xn, wg_ref[...], preferred_element_type=F32)
    up = jnp.dot(xn, wu_ref[...], preferred_element_type=F32)
    act = ((gate * jax.nn.sigmoid(gate)) * up).astype(BF16)
    acc_ref[...] += jnp.dot(act, wo_ref[...], preferred_element_type=F32)

    @pl.when(j == pl.num_programs(1) - 1)
    def _():
        h = x_ref[...] + 0.5 * acc_ref[...]
        if final:
            o_ref[...] = _rms(h, g2_ref[...])
        else:
            h_ref[...] = h
            u_ref[...] = _rms(h, g2_ref[...]).astype(BF16)


def _ffn(x, g, w_gu, wo, g2, *, final):
    n = x.shape[0]
    tm = FFN_TM
    nf = D_FF_PAD // FF_TILE
    row = lambda i, j: (i, 0)
    in_specs = [
        pl.BlockSpec((tm, D_MODEL), row),
        pl.BlockSpec((1, D_MODEL), lambda i, j: (0, 0)),
        pl.BlockSpec((None, None, D_MODEL, FF_TILE), lambda i, j: (0, j, 0, 0)),
        pl.BlockSpec((None, None, D_MODEL, FF_TILE), lambda i, j: (1, j, 0, 0)),
        pl.BlockSpec((FF_TILE, D_MODEL), lambda i, j: (j, 0)),
        pl.BlockSpec((1, D_MODEL), lambda i, j: (0, 0)),
    ]
    if final:
        out_shape = jax.ShapeDtypeStruct((n, D_MODEL), F32)
        out_specs = pl.BlockSpec((tm, D_MODEL), row)
    else:
        out_shape = (jax.ShapeDtypeStruct((n, D_MODEL), F32), jax.ShapeDtypeStruct((n, D_MODEL), BF16))
        out_specs = (pl.BlockSpec((tm, D_MODEL), row), pl.BlockSpec((tm, D_MODEL), row))
    return pl.pallas_call(
        functools.partial(_ffn_kernel, final=final),
        grid=(n // tm, nf),
        in_specs=in_specs,
        out_specs=out_specs,
        out_shape=out_shape,
        scratch_shapes=[pltpu.VMEM((tm, D_MODEL), BF16), pltpu.VMEM((tm, D_MODEL), F32)],
        compiler_params=_params(2),
        name="ffn_final" if final else "ffn_first",
    )(x, g, w_gu, w_gu, wo, g2)


def _rope(y, cos2, sin2):
    lane = lax.broadcasted_iota(jnp.int32, y.shape, 1)
    partner = jnp.where(lane % 2 == 0, pltpu.roll(y, HEAD_DIM - 1, 1), pltpu.roll(y, 1, 1))
    return y * cos2 + partner * sin2


def _row_chunks(n_rows):
    return [slice(r, r + PROJ_RC) for r in range(0, n_rows, PROJ_RC)]


def _proj_kernel(u_ref, w_ref, gq_ref, gk_ref, cos_ref, sin_ref, o_ref):
    j = pl.program_id(1)
    heads_per_tile = PROJ_TN // HEAD_DIM
    qa_tiles = WIDTH_QA // PROJ_TN
    kva_tile = qa_tiles
    qb_lo = (CB_QB * HEAD_DIM) // PROJ_TN
    qb_hi = (CB_KB * HEAD_DIM) // PROJ_TN

    def normed_rope(yh, g_ref, rows):
        return _rope(_rms(yh, g_ref[...]), cos_ref[rows, :], sin_ref[rows, :])

    def for_heads(epilogue):
        for rows in _row_chunks(u_ref.shape[0]):
            y = jnp.dot(u_ref[rows, :], w_ref[...], preferred_element_type=F32)
            for hh in range(heads_per_tile):
                o_ref[0, hh, rows, :] = epilogue(hh, rows, y[:, hh * HEAD_DIM:(hh + 1) * HEAD_DIM]).astype(BF16)

    @pl.when(j < qa_tiles)
    def _():
        for_heads(lambda hh, rows, yh: normed_rope(yh, gq_ref, rows) * QK_SCALE)

    @pl.when(j == kva_tile)
    def _():
        for_heads(lambda hh, rows, yh: normed_rope(yh, gk_ref, rows) if hh < N_KV_HEADS_A else yh)

    @pl.when((j >= qb_lo) & (j < qb_hi))
    def _():
        for_heads(lambda hh, rows, yh: yh * QK_SCALE)

    @pl.when(j >= qb_hi)
    def _():
        for_heads(lambda hh, rows, yh: yh)


def _proj(u, w_blocks, gq, gk, cos2, sin2, batch, seq):
    n = u.shape[0]
    tm = min(PROJ_TM, seq)
    pos_tiles = seq // tm
    heads_per_tile = PROJ_TN // HEAD_DIM
    return pl.pallas_call(
        _proj_kernel,
        grid=(n // tm, QKV_WIDTH // PROJ_TN),
        in_specs=[
            pl.BlockSpec((tm, D_MODEL), lambda i, j: (i, 0)),
            pl.BlockSpec((None, None, D_MODEL, PROJ_TN), lambda i, j: (0, j, 0, 0)),
            pl.BlockSpec((1, HEAD_DIM), lambda i, j: (0, 0)),
            pl.BlockSpec((1, HEAD_DIM), lambda i, j: (0, 0)),
            pl.BlockSpec((tm, HEAD_DIM), lambda i, j: (i % pos_tiles, 0)),
            pl.BlockSpec((tm, HEAD_DIM), lambda i, j: (i % pos_tiles, 0)),
        ],
        out_specs=pl.BlockSpec((1, heads_per_tile, tm, HEAD_DIM),
                               lambda i, j: (i // pos_tiles, j, i % pos_tiles, 0)),
        out_shape=jax.ShapeDtypeStruct((batch, N_QKV_HEADS, seq, HEAD_DIM), BF16),
        compiler_params=_params(2),
        name="proj_qkv",
    )(u, w_blocks, gq, gk, cos2, sin2)


def _gate_kernel(u_ref, w_ref, b_ref, o_ref):
    for rows in _row_chunks(u_ref.shape[0]):
        y = jnp.dot(u_ref[rows, :], w_ref[...], preferred_element_type=F32)
        o_ref[rows, :] = jax.nn.sigmoid(y + b_ref[...])


def _gates(u, w_blocks, b_gate, seq):
    n = u.shape[0]
    tm = min(PROJ_TM, seq)
    first = QKV_WIDTH // PROJ_TN
    return pl.pallas_call(
        _gate_kernel,
        grid=(n // tm, GATE_WIDTH // PROJ_TN),
        in_specs=[
            pl.BlockSpec((tm, D_MODEL), lambda i, j: (i, 0)),
            pl.BlockSpec((None, None, D_MODEL, PROJ_TN), lambda i, j: (0, first + j, 0, 0)),
            pl.BlockSpec((1, PROJ_TN), lambda i, j: (0, j)),
        ],
        out_specs=pl.BlockSpec((tm, PROJ_TN), lambda i, j: (i, j)),
        out_shape=jax.ShapeDtypeStruct((n, GATE_WIDTH), F32),
        compiler_params=_params(2),
        name="proj_gates",
    )(u, w_blocks, b_gate)


def _gqa_kernel(q_ref, k_ref, v_ref, o_ref, *, seq):
    ts = GQA_TQ // GQA_STREAMS
    rows = GROUP_A * ts
    qs, ms, ls, accs = [], [], [], []
    for t in range(GQA_STREAMS):
        tok = slice(t * ts, (t + 1) * ts)
        qs.append(jnp.concatenate([q_ref[0, g, tok, :] for g in range(GROUP_A)], axis=0))
        ms.append(jnp.full((1, rows), -jnp.inf, F32))
        ls.append(jnp.zeros((1, rows), F32))
        accs.append(jnp.zeros((HEAD_DIM, rows), F32))
    n_chunks = seq // GQA_TK

    def scores(c, t):
        k = k_ref[0, 0, c * GQA_TK:(c + 1) * GQA_TK, :]
        return lax.dot_general(k, qs[t], (((1,), (1,)), ((), ())), preferred_element_type=F32)

    work = [(c, t) for c in range(n_chunks) for t in range(GQA_STREAMS)]
    st_next = scores(*work[0])
    for i, (c, t) in enumerate(work):
        st = st_next
        if i + 1 < len(work):
            st_next = scores(*work[i + 1])
        v = v_ref[0, 0, c * GQA_TK:(c + 1) * GQA_TK, :]
        m_new = jnp.maximum(ms[t], jnp.max(st, axis=0, keepdims=True))
        alpha = jnp.exp2(ms[t] - m_new)
        p = jnp.exp2(st - m_new)
        ls[t] = alpha * ls[t] + jnp.sum(p, axis=0, keepdims=True)
        pv = lax.dot_general(v, p.astype(BF16), (((0,), (0,)), ((), ())), preferred_element_type=F32)
        accs[t] = alpha * accs[t] + pv
        ms[t] = m_new
    for t in range(GQA_STREAMS):
        o = (accs[t] / ls[t]).T
        for g in range(GROUP_A):
            o_ref[0, t * ts:(t + 1) * ts, g * HEAD_DIM:(g + 1) * HEAD_DIM] = o[g * ts:(g + 1) * ts].astype(BF16)


def _gqa(qkv, batch, seq):
    gw = GROUP_A * HEAD_DIM
    return pl.pallas_call(
        functools.partial(_gqa_kernel, seq=seq),
        grid=(batch, N_KV_HEADS_A, seq // GQA_TQ),
        in_specs=[
            pl.BlockSpec((1, GROUP_A, GQA_TQ, HEAD_DIM), lambda b, h, i: (b, h, i, 0)),
            pl.BlockSpec((1, 1, seq, HEAD_DIM), lambda b, h, i: (b, CB_KA + h, 0, 0)),
            pl.BlockSpec((1, 1, seq, HEAD_DIM), lambda b, h, i: (b, CB_VA + h, 0, 0)),
        ],
        out_specs=pl.BlockSpec((1, GQA_TQ, gw), lambda b, h, i: (b, i, h)),
        out_shape=jax.ShapeDtypeStruct((batch, seq, WIDTH_QA), BF16),
        compiler_params=_params(3),
        name="gqa",
    )(qkv, qkv, qkv)


def _nbr_pair_table(rpb):
    qc = np.arange(GRID_W)[:, None]
    kc = np.arange(GRID_W)[None, :]
    cs = np.clip(qc - WIN_COLS // 2, 0, GRID_W - WIN_COLS)
    col_ok = (kc >= cs) & (kc < cs + WIN_COLS)
    dc = np.clip(kc - qc + WIN_COLS - 1, 0, 2 * WIN_COLS - 2)
    onehot = (dc.reshape(1, -1) == np.arange(2 * WIN_COLS - 1)[:, None]).astype(np.float32)
    n_dr = 2 * WIN_ROWS - 1
    picked = jnp.dot(rpb.astype(F32).reshape(N_HEADS_B * n_dr, -1), jnp.asarray(onehot),
                     precision=lax.Precision.HIGHEST).reshape(N_HEADS_B, n_dr, GRID_W, GRID_W)
    per_dr = jnp.where(col_ok[None, None], picked * LOG2E, NEG)
    pad = NBR_MAX_OFF - (WIN_ROWS - 1)
    blocked = jnp.full((N_HEADS_B, pad, GRID_W, GRID_W), NEG, F32)
    ext = jnp.concatenate([blocked, per_dr, blocked], axis=1)
    return jnp.concatenate([ext[:, :-1], ext[:, 1:]], axis=-1)


def _nbr_row_masks(rows):
    n_groups = rows // NBR_QR
    assert n_groups >= 3
    masks = []
    for gg in (0, 1, n_groups - 1):
        r = gg * NBR_QR + np.arange(NBR_QR)
        ws = int(np.clip(gg * NBR_QR - WIN_ROWS // 2, 0, rows - NBR_KR))
        rs = np.clip(r - WIN_ROWS // 2, 0, rows - WIN_ROWS)
        rk = ws + np.arange(NBR_KR)
        ok = (rk[None, :] >= rs[:, None]) & (rk[None, :] < rs[:, None] + WIN_ROWS)
        assert ok.sum(axis=1).min() == WIN_ROWS
        m = np.where(ok, 0.0, NEG).astype(np.float32)
        masks.append(np.repeat(np.repeat(m, GRID_W, axis=0), GRID_W, axis=1))
    return jnp.asarray(np.stack(masks))


def _nbr_kernel(q_ref, *refs, rows):
    k_refs = refs[:NBR_HEADS]
    v_refs = refs[NBR_HEADS:2 * NBR_HEADS]
    pair_ref, mask_ref, o_ref = refs[2 * NBR_HEADS:]
    rb = pl.program_id(2)
    n_groups = rows // NBR_QR
    tiles = []
    for g in range(NBR_GROUPS):
        gg = rb * NBR_GROUPS + g
        ws = jnp.clip(gg * NBR_QR - WIN_ROWS // 2, 0, rows - NBR_KR)
        pattern = jnp.where(gg == 0, 0, jnp.where(gg == n_groups - 1, 2, 1))
        d0 = ws - gg * NBR_QR + NBR_MAX_OFF
        start = pl.multiple_of(ws * GRID_W, GRID_W)
        qrows = slice(g * NBR_GQ, (g + 1) * NBR_GQ)
        for hh in range(NBR_HEADS):
            tiles.append((hh, qrows, start, d0, pattern))

    def scores(tile):
        hh, qrows, start, _, _ = tile
        k = k_refs[hh][0, 0, pl.ds(start, NBR_GK), :]
        return lax.dot_general(q_ref[0, hh, qrows, :], k, (((1,), (1,)), ((), ())), preferred_element_type=F32)

    s_next = scores(tiles[0])
    for i, (hh, qrows, start, d0, pattern) in enumerate(tiles):
        s = s_next
        if i + 1 < len(tiles):
            s_next = scores(tiles[i + 1])
        bias = jnp.concatenate(
            [jnp.concatenate([pair_ref[hh, d0 + 2 * c - rq] for c in range(NBR_KR // 2)], axis=1)
             for rq in range(NBR_QR)], axis=0)
        s = s + bias + mask_ref[pattern]
        m = jnp.max(s, axis=-1, keepdims=True)
        p = jnp.exp2(s - m)
        l = jnp.sum(p, axis=-1, keepdims=True)
        v = v_refs[hh][0, 0, pl.ds(start, NBR_GK), :]
        o = jnp.dot(p.astype(BF16), v, preferred_element_type=F32) / l
        o_ref[0, qrows, hh * HEAD_DIM:(hh + 1) * HEAD_DIM] = o.astype(BF16)


def _nbr(qkv, pair, rowmask, batch, seq):
    rows = seq // GRID_W
    n_off = pair.shape[1]
    hw = NBR_HEADS * HEAD_DIM

    def head_block(cb, hh):
        return pl.BlockSpec((1, 1, seq, HEAD_DIM), lambda b, h, i: (b, cb + NBR_HEADS * h + hh, 0, 0))

    return pl.pallas_call(
        functools.partial(_nbr_kernel, rows=rows),
        grid=(batch, N_HEADS_B // NBR_HEADS, rows // NBR_ROWS),
        in_specs=[pl.BlockSpec((1, NBR_HEADS, NBR_TQ, HEAD_DIM), lambda b, h, i: (b, CB_QB // NBR_HEADS + h, i, 0))]
        + [head_block(CB_KB, hh) for hh in range(NBR_HEADS)]
        + [head_block(CB_VB, hh) for hh in range(NBR_HEADS)]
        + [pl.BlockSpec((NBR_HEADS, n_off, GRID_W, 2 * GRID_W), lambda b, h, i: (h, 0, 0, 0)),
           pl.BlockSpec((3, NBR_GQ, NBR_GK), lambda b, h, i: (0, 0, 0))],
        out_specs=pl.BlockSpec((1, NBR_TQ, hw), lambda b, h, i: (b, i, h)),
        out_shape=jax.ShapeDtypeStruct((batch, seq, WIDTH_B), BF16),
        compiler_params=_params(3),
        name="nbr",
    )(qkv, *([qkv] * (2 * NBR_HEADS)), pair, rowmask)


def _merge_kernel(h_ref, ya_ref, yb_ref, gate_ref, wa_ref, wb_ref, wo_ref, o_ref):
    for r in range(0, MERGE_TM, MERGE_RC):
        rows = slice(r, r + MERGE_RC)
        ta = jnp.dot(ya_ref[rows, :], wa_ref[...], preferred_element_type=F32)
        tb = jnp.dot(yb_ref[rows, :], wb_ref[...], preferred_element_type=F32)
        merged = gate_ref[rows, :D_MODEL] * ta + gate_ref[rows, D_MODEL:] * tb
        o_ref[rows, :] = h_ref[rows, :] + jnp.dot(merged.astype(BF16), wo_ref[...], preferred_element_type=F32)


def _merge(h, ya, yb, gates, wa, wb, wo):
    n = h.shape[0]
    tm = MERGE_TM
    row = lambda i: (i, 0)
    fixed = lambda i: (0, 0)
    return pl.pallas_call(
        _merge_kernel,
        grid=(n // tm,),
        in_specs=[
            pl.BlockSpec((tm, D_MODEL), row),
            pl.BlockSpec((tm, WIDTH_QA), row),
            pl.BlockSpec((tm, WIDTH_B), row),
            pl.BlockSpec((tm, GATE_WIDTH), row),
            pl.BlockSpec((WIDTH_QA, D_MODEL), fixed, pipeline_mode=pl.Buffered(1)),
            pl.BlockSpec((WIDTH_B, D_MODEL), fixed, pipeline_mode=pl.Buffered(1)),
            pl.BlockSpec((D_MODEL, D_MODEL), fixed, pipeline_mode=pl.Buffered(1)),
        ],
        out_specs=pl.BlockSpec((tm, D_MODEL), row),
        out_shape=jax.ShapeDtypeStruct((n, D_MODEL), F32),
        compiler_params=_params(1),
        name="merge_out",
    )(h, ya, yb, gates, wa, wb, wo)


def _rope_tables(seq):
    t = np.arange(seq)
    row = (t // GRID_W).astype(np.float32)
    col = (t % GRID_W).astype(np.float32)
    n_pairs_axis = HEAD_DIM // 4
    inv = (ROPE_THETA ** (-np.arange(n_pairs_axis, dtype=np.float32) / n_pairs_axis)).astype(np.float32)
    ang = np.concatenate([row[:, None] * inv[None], col[:, None] * inv[None]], axis=-1)
    cos = np.cos(ang).astype(np.float32)
    sin = np.sin(ang).astype(np.float32)
    cos2 = np.repeat(cos, 2, axis=-1)
    sin2 = np.stack([-sin, sin], axis=-1).reshape(seq, HEAD_DIM)
    return jnp.asarray(cos2), jnp.asarray(sin2)


def _ffn_weights(w_in, w_out):
    return _cast_cols(w_in, 2, FF_TILE), _cast_rows(w_out, D_FF_PAD)


def kernel(x_prompt, x_sample, g_ffn1, w_ffn1_in, w_ffn1_out, g_mix, w_in, b_gate, g_q_a, g_k_a, rpb_b,
           w_branch_a, w_branch_b, w_out, g_ffn2, w_ffn2_in, w_ffn2_out, g_final):
    assert g_ffn1.shape[0] == 1, "single-layer stack"
    w1 = _ffn_weights(w_ffn1_in[0], w_ffn1_out[0])
    w2 = _ffn_weights(w_ffn2_in[0], w_ffn2_out[0])
    w_proj = _cast_cols(w_in[0], 1, PROJ_TN)
    wa = w_branch_a[0].astype(BF16)
    wb = w_branch_b[0].astype(BF16)
    wo = w_out[0].astype(BF16)
    g1 = g_ffn1[0][None, :]
    gm = g_mix[0][None, :]
    g2 = g_ffn2[0][None, :]
    gf = g_final[None, :]
    gq = g_q_a[0][None, :]
    gk = g_k_a[0][None, :]
    bg = b_gate[0][None, :]
    pair = _nbr_pair_table(rpb_b[0])

    def trunk(x):
        batch, seq, _ = x.shape
        n = batch * seq
        cos2, sin2 = _rope_tables(seq)
        rowmask = _nbr_row_masks(seq // GRID_W)
        h, u = _ffn(x.reshape(n, D_MODEL), g1, *w1, gm, final=False)
        qkv = _proj(u, w_proj, gq, gk, cos2, sin2, batch, seq)
        gates = _gates(u, w_proj, bg, seq)
        ya = _gqa(qkv, batch, seq).reshape(n, WIDTH_QA)
        yb = _nbr(qkv, pair, rowmask, batch, seq).reshape(n, WIDTH_B)
        h2 = _merge(h, ya, yb, gates, wa, wb, wo)
        out = _ffn(h2, g2, *w2, gf, final=True)
        return out.reshape(batch, seq, D_MODEL)

    return (trunk(x_prompt), trunk(x_sample))
```

```python
import functools

import numpy as np
import jax
import jax.numpy as jnp
from jax import lax
from jax.experimental import pallas as pl
from jax.experimental.pallas import tpu as pltpu

D_MODEL = 2048
HEAD_DIM = 128
N_HEADS_A = 8
N_KV_HEADS_A = 2
GROUP_A = N_HEADS_A // N_KV_HEADS_A
N_HEADS_B = 8
GRID_W = 64
WIN_ROWS = 8
WIN_COLS = 16
D_FF = 5504
ROPE_THETA = 10000.0
EPS = 1e-6
NEG = -1e30

WIDTH_QA = N_HEADS_A * HEAD_DIM
WIDTH_KVA = N_KV_HEADS_A * HEAD_DIM
WIDTH_B = N_HEADS_B * HEAD_DIM
QKV_WIDTH = WIDTH_QA + 2 * WIDTH_KVA + 3 * WIDTH_B
GATE_WIDTH = 2 * D_MODEL
IN_WIDTH = QKV_WIDTH + GATE_WIDTH
N_QKV_HEADS = QKV_WIDTH // HEAD_DIM

CB_QA = 0
CB_KA = WIDTH_QA // HEAD_DIM
CB_VA = CB_KA + N_KV_HEADS_A
CB_QB = CB_VA + N_KV_HEADS_A
CB_KB = CB_QB + N_HEADS_B
CB_VB = CB_KB + N_HEADS_B

V7X_VMEM_BYTES = 64 * 1024 * 1024
VMEM_LIMIT = V7X_VMEM_BYTES - 8 * 1024 * 1024

FF_TILE = 512
FF_CHUNKS = -(-D_FF // FF_TILE)
FF_PER_STEP = 2
FF_STEPS = -(-FF_CHUNKS // FF_PER_STEP)
FF_LAST = FF_CHUNKS - FF_PER_STEP * (FF_STEPS - 1)
D_FF_PAD = FF_STEPS * FF_PER_STEP * FF_TILE
FFN_TM = 512
CAST_ROWS = 128
CAST_COLS = 512
PROJ_TM = 2048
PROJ_TN = 512
PROJ_RC = 512
MERGE_TM = 512
MERGE_RC = 256
GQA_TQ = 512
GQA_STREAMS = 4
GQA_TK = 512
NBR_ROWS = 16
NBR_QR = 4
NBR_GROUPS = NBR_ROWS // NBR_QR
NBR_KR = 12
NBR_HEADS = 2
NBR_TQ = NBR_ROWS * GRID_W
NBR_GQ = NBR_QR * GRID_W
NBR_GK = NBR_KR * GRID_W
NBR_MAX_OFF = NBR_KR - 1

LOG2E = float(np.log2(np.e))
QK_SCALE = HEAD_DIM ** -0.5 * LOG2E

F32 = jnp.float32
BF16 = jnp.bfloat16


def _params(n_axes):
    return pltpu.CompilerParams(dimension_semantics=("arbitrary",) * n_axes,
                                vmem_limit_bytes=VMEM_LIMIT)


def _rms(xf, g):
    ms = jnp.mean(xf * xf, axis=-1, keepdims=True)
    return xf * lax.rsqrt(ms + EPS) * g


def _cast_cols_kernel(w_ref, o_ref, *, valid):
    n_chunks, rows, tn = o_ref.shape
    for c in range(n_chunks):
        lo = c * tn
        width = max(0, min(tn, valid - lo))
        if width > 0:
            o_ref[c, :, :width] = w_ref[:, lo:lo + width].astype(BF16)
        if width < tn:
            o_ref[c, :, width:] = jnp.zeros((rows, tn - width), BF16)


def _cast_cols(w, n_parts, tn, n_chunks):
    r, total = w.shape
    valid = total // n_parts
    assert n_chunks * tn >= valid
    return pl.pallas_call(
        functools.partial(_cast_cols_kernel, valid=valid),
        grid=(n_parts, r // CAST_ROWS),
        in_specs=[pl.BlockSpec((CAST_ROWS, valid), lambda p, i: (i, p))],
        out_specs=pl.BlockSpec((None, n_chunks, CAST_ROWS, tn), lambda p, i: (p, 0, i, 0)),
        out_shape=jax.ShapeDtypeStruct((n_parts, n_chunks, r, tn), BF16),
        compiler_params=_params(2),
        name="cast_cols",
    )(w)


def _cast_rows_kernel(w_ref, o_ref):
    r = w_ref.shape[0]
    o_ref[:r, :] = w_ref[...].astype(BF16)
    o_ref[r:, :] = jnp.zeros((o_ref.shape[0] - r, o_ref.shape[1]), BF16)


def _cast_rows(w, rows_out):
    r, c = w.shape
    return pl.pallas_call(
        _cast_rows_kernel,
        grid=(c // CAST_COLS,),
        in_specs=[pl.BlockSpec((r, CAST_COLS), lambda i: (0, i))],
        out_specs=pl.BlockSpec((rows_out, CAST_COLS), lambda i: (0, i)),
        out_shape=jax.ShapeDtypeStruct((rows_out, c), BF16),
        compiler_params=_params(1),
        name="cast_rows",
    )(w)


def _ffn_kernel(x_ref, g_ref, wg_ref, wu_ref, wo_ref, g2_ref, *refs, final):
    if final:
        acc_ref, xn_ref = refs
    else:
        acc_ref, u_ref, xn_ref = refs
    j = pl.program_id(1)
    last = pl.num_programs(1) - 1

    @pl.when(j == 0)
    def _():
        xn_ref[...] = _rms(x_ref[...], g_ref[...]).astype(BF16)
        acc_ref[...] = jnp.zeros_like(acc_ref)

    def chunk(c):
        xn = xn_ref[...]
        gate = jnp.dot(xn, wg_ref[c], preferred_element_type=F32)
        up = jnp.dot(xn, wu_ref[c], preferred_element_type=F32)
        act = ((gate * jax.nn.sigmoid(gate)) * up).astype(BF16)
        acc_ref[...] += jnp.dot(act, wo_ref[c * FF_TILE:(c + 1) * FF_TILE, :], preferred_element_type=F32)

    @pl.when(j < last)
    def _():
        for c in range(FF_PER_STEP):
            chunk(c)

    @pl.when(j == last)
    def _():
        for c in range(FF_LAST):
            chunk(c)
        h = x_ref[...] + 0.5 * acc_ref[...]
        if final:
            acc_ref[...] = _rms(h, g2_ref[...])
        else:
            acc_ref[...] = h
            u_ref[...] = _rms(h, g2_ref[...]).astype(BF16)


def _ffn(x, g, w_gu, wo, g2, *, final):
    n = x.shape[0]
    tm = FFN_TM
    row = lambda i, j: (i, 0)
    in_specs = [
        pl.BlockSpec((tm, D_MODEL), row),
        pl.BlockSpec((1, D_MODEL), lambda i, j: (0, 0)),
        pl.BlockSpec((None, FF_PER_STEP, D_MODEL, FF_TILE), lambda i, j: (0, j, 0, 0)),
        pl.BlockSpec((None, FF_PER_STEP, D_MODEL, FF_TILE), lambda i, j: (1, j, 0, 0)),
        pl.BlockSpec((FF_PER_STEP * FF_TILE, D_MODEL), lambda i, j: (j, 0)),
        pl.BlockSpec((1, D_MODEL), lambda i, j: (0, 0)),
    ]
    if final:
        out_shape = jax.ShapeDtypeStruct((n, D_MODEL), F32)
        out_specs = pl.BlockSpec((tm, D_MODEL), row)
    else:
        out_shape = (jax.ShapeDtypeStruct((n, D_MODEL), F32), jax.ShapeDtypeStruct((n, D_MODEL), BF16))
        out_specs = (pl.BlockSpec((tm, D_MODEL), row), pl.BlockSpec((tm, D_MODEL), row))
    return pl.pallas_call(
        functools.partial(_ffn_kernel, final=final),
        grid=(n // tm, FF_STEPS),
        in_specs=in_specs,
        out_specs=out_specs,
        out_shape=out_shape,
        scratch_shapes=[pltpu.VMEM((tm, D_MODEL), BF16)],
        compiler_params=_params(2),
        name="ffn_final" if final else "ffn_first",
    )(x, g, w_gu, w_gu, wo, g2)


def _rope(y, cos2, sin2):
    lane = lax.broadcasted_iota(jnp.int32, y.shape, 1)
    partner = jnp.where(lane % 2 == 0, pltpu.roll(y, HEAD_DIM - 1, 1), pltpu.roll(y, 1, 1))
    return y * cos2 + partner * sin2


def _row_chunks(n_rows):
    return [slice(r, r + PROJ_RC) for r in range(0, n_rows, PROJ_RC)]


def _proj_kernel(u_ref, w_ref, gq_ref, gk_ref, cos_ref, sin_ref, o_ref):
    j = pl.program_id(1)
    heads_per_tile = PROJ_TN // HEAD_DIM
    qa_tiles = WIDTH_QA // PROJ_TN
    kva_tile = qa_tiles
    qb_lo = (CB_QB * HEAD_DIM) // PROJ_TN
    qb_hi = (CB_KB * HEAD_DIM) // PROJ_TN

    def normed_rope(yh, g_ref, rows):
        return _rope(_rms(yh, g_ref[...]), cos_ref[rows, :], sin_ref[rows, :])

    def for_heads(epilogue):
        for rows in _row_chunks(u_ref.shape[0]):
            y = jnp.dot(u_ref[rows, :], w_ref[...], preferred_element_type=F32)
            for hh in range(heads_per_tile):
                o_ref[0, hh, rows, :] = epilogue(hh, rows, y[:, hh * HEAD_DIM:(hh + 1) * HEAD_DIM]).astype(BF16)

    @pl.when(j < qa_tiles)
    def _():
        for_heads(lambda hh, rows, yh: normed_rope(yh, gq_ref, rows) * QK_SCALE)

    @pl.when(j == kva_tile)
    def _():
        for_heads(lambda hh, rows, yh: normed_rope(yh, gk_ref, rows) if hh < N_KV_HEADS_A else yh)

    @pl.when((j >= qb_lo) & (j < qb_hi))
    def _():
        for_heads(lambda hh, rows, yh: yh * QK_SCALE)

    @pl.when(j >= qb_hi)
    def _():
        for_heads(lambda hh, rows, yh: yh)


def _proj(u, w_blocks, gq, gk, cos2, sin2, batch, seq):
    n = u.shape[0]
    tm = min(PROJ_TM, seq)
    pos_tiles = seq // tm
    heads_per_tile = PROJ_TN // HEAD_DIM
    return pl.pallas_call(
        _proj_kernel,
        grid=(n // tm, QKV_WIDTH // PROJ_TN),
        in_specs=[
            pl.BlockSpec((tm, D_MODEL), lambda i, j: (i, 0)),
            pl.BlockSpec((None, None, D_MODEL, PROJ_TN), lambda i, j: (0, j, 0, 0)),
            pl.BlockSpec((1, HEAD_DIM), lambda i, j: (0, 0)),
            pl.BlockSpec((1, HEAD_DIM), lambda i, j: (0, 0)),
            pl.BlockSpec((tm, HEAD_DIM), lambda i, j: (i % pos_tiles, 0)),
            pl.BlockSpec((tm, HEAD_DIM), lambda i, j: (i % pos_tiles, 0)),
        ],
        out_specs=pl.BlockSpec((1, heads_per_tile, tm, HEAD_DIM),
                               lambda i, j: (i // pos_tiles, j, i % pos_tiles, 0)),
        out_shape=jax.ShapeDtypeStruct((batch, N_QKV_HEADS, seq, HEAD_DIM), BF16),
        compiler_params=_params(2),
        name="proj_qkv",
    )(u, w_blocks, gq, gk, cos2, sin2)


def _gate_kernel(u_ref, w_ref, b_ref, o_ref):
    for rows in _row_chunks(u_ref.shape[0]):
        y = jnp.dot(u_ref[rows, :], w_ref[...], preferred_element_type=F32)
        o_ref[rows, :] = jax.nn.sigmoid(y + b_ref[...])


def _gates(u, w_blocks, b_gate, seq):
    n = u.shape[0]
    tm = min(PROJ_TM, seq)
    first = QKV_WIDTH // PROJ_TN
    return pl.pallas_call(
        _gate_kernel,
        grid=(n // tm, GATE_WIDTH // PROJ_TN),
        in_specs=[
            pl.BlockSpec((tm, D_MODEL), lambda i, j: (i, 0)),
            pl.BlockSpec((None, None, D_MODEL, PROJ_TN), lambda i, j: (0, first + j, 0, 0)),
            pl.BlockSpec((1, PROJ_TN), lambda i, j: (0, j)),
        ],
        out_specs=pl.BlockSpec((tm, PROJ_TN), lambda i, j: (i, j)),
        out_shape=jax.ShapeDtypeStruct((n, GATE_WIDTH), F32),
        compiler_params=_params(2),
        name="proj_gates",
    )(u, w_blocks, b_gate)


def _gqa_kernel(q_ref, k_ref, v_ref, o_ref, *, seq):
    ts = GQA_TQ // GQA_STREAMS
    rows = GROUP_A * ts
    qs, ms, ls, accs = [], [], [], []
    for t in range(GQA_STREAMS):
        tok = slice(t * ts, (t + 1) * ts)
        qs.append(jnp.concatenate([q_ref[0, g, tok, :] for g in range(GROUP_A)], axis=0))
        ms.append(jnp.full((1, rows), -jnp.inf, F32))
        ls.append(jnp.zeros((1, rows), F32))
        accs.append(jnp.zeros((HEAD_DIM, rows), F32))
    n_chunks = seq // GQA_TK

    def scores(c, t):
        k = k_ref[0, 0, c * GQA_TK:(c + 1) * GQA_TK, :]
        return lax.dot_general(k, qs[t], (((1,), (1,)), ((), ())), preferred_element_type=F32)

    work = [(c, t) for c in range(n_chunks) for t in range(GQA_STREAMS)]
    st_next = scores(*work[0])
    for i, (c, t) in enumerate(work):
        st = st_next
        if i + 1 < len(work):
            st_next = scores(*work[i + 1])
        v = v_ref[0, 0, c * GQA_TK:(c + 1) * GQA_TK, :]
        m_new = jnp.maximum(ms[t], jnp.max(st, axis=0, keepdims=True))
        alpha = jnp.exp2(ms[t] - m_new)
        p = jnp.exp2(st - m_new)
        ls[t] = alpha * ls[t] + jnp.sum(p, axis=0, keepdims=True)
        pv = lax.dot_general(v, p.astype(BF16), (((0,), (0,)), ((), ())), preferred_element_type=F32)
        accs[t] = alpha * accs[t] + pv
        ms[t] = m_new
    for t in range(GQA_STREAMS):
        o = (accs[t] / ls[t]).T
        for g in range(GROUP_A):
            o_ref[0, t * ts:(t + 1) * ts, g * HEAD_DIM:(g + 1) * HEAD_DIM] = o[g * ts:(g + 1) * ts].astype(BF16)


def _gqa(qkv, batch, seq):
    gw = GROUP_A * HEAD_DIM
    return pl.pallas_call(
        functools.partial(_gqa_kernel, seq=seq),
        grid=(batch, N_KV_HEADS_A, seq // GQA_TQ),
        in_specs=[
            pl.BlockSpec((1, GROUP_A, GQA_TQ, HEAD_DIM), lambda b, h, i: (b, h, i, 0)),
            pl.BlockSpec((1, 1, seq, HEAD_DIM), lambda b, h, i: (b, CB_KA + h, 0, 0)),
            pl.BlockSpec((1, 1, seq, HEAD_DIM), lambda b, h, i: (b, CB_VA + h, 0, 0)),
        ],
        out_specs=pl.BlockSpec((1, GQA_TQ, gw), lambda b, h, i: (b, i, h)),
        out_shape=jax.ShapeDtypeStruct((batch, seq, WIDTH_QA), BF16),
        compiler_params=_params(3),
        name="gqa",
    )(qkv, qkv, qkv)


def _nbr_pair_table(rpb):
    qc = np.arange(GRID_W)[:, None]
    kc = np.arange(GRID_W)[None, :]
    cs = np.clip(qc - WIN_COLS // 2, 0, GRID_W - WIN_COLS)
    col_ok = (kc >= cs) & (kc < cs + WIN_COLS)
    dc = np.clip(kc - qc + WIN_COLS - 1, 0, 2 * WIN_COLS - 2)
    onehot = (dc.reshape(1, -1) == np.arange(2 * WIN_COLS - 1)[:, None]).astype(np.float32)
    n_dr = 2 * WIN_ROWS - 1
    picked = jnp.dot(rpb.astype(F32).reshape(N_HEADS_B * n_dr, -1), jnp.asarray(onehot),
                     precision=lax.Precision.HIGHEST).reshape(N_HEADS_B, n_dr, GRID_W, GRID_W)
    per_dr = jnp.where(col_ok[None, None], picked * LOG2E, NEG)
    pad = NBR_MAX_OFF - (WIN_ROWS - 1)
    blocked = jnp.full((N_HEADS_B, pad, GRID_W, GRID_W), NEG, F32)
    ext = jnp.concatenate([blocked, per_dr, blocked], axis=1)
    return jnp.concatenate([ext[:, :-1], ext[:, 1:]], axis=-1)


def _nbr_row_masks(rows):
    n_groups = rows // NBR_QR
    assert n_groups >= 3
    masks = []
    for gg in (0, 1, n_groups - 1):
        r = gg * NBR_QR + np.arange(NBR_QR)
        ws = int(np.clip(gg * NBR_QR - WIN_ROWS // 2, 0, rows - NBR_KR))
        rs = np.clip(r - WIN_ROWS // 2, 0, rows - WIN_ROWS)
        rk = ws + np.arange(NBR_KR)
        ok = (rk[None, :] >= rs[:, None]) & (rk[None, :] < rs[:, None] + WIN_ROWS)
        assert ok.sum(axis=1).min() == WIN_ROWS
        m = np.where(ok, 0.0, NEG).astype(np.float32)
        masks.append(np.repeat(np.repeat(m, GRID_W, axis=0), GRID_W, axis=1))
    return jnp.asarray(np.stack(masks))


def _nbr_kernel(q_ref, *refs, rows):
    k_refs = refs[:NBR_HEADS]
    v_refs = refs[NBR_HEADS:2 * NBR_HEADS]
    pair_ref, mask_ref, o_ref = refs[2 * NBR_HEADS:]
    rb = pl.program_id(2)
    n_groups = rows // NBR_QR
    tiles = []
    for g in range(NBR_GROUPS):
        gg = rb * NBR_GROUPS + g
        ws = jnp.clip(gg * NBR_QR - WIN_ROWS // 2, 0, rows - NBR_KR)
        pattern = jnp.where(gg == 0, 0, jnp.where(gg == n_groups - 1, 2, 1))
        d0 = ws - gg * NBR_QR + NBR_MAX_OFF
        start = pl.multiple_of(ws * GRID_W, GRID_W)
        qrows = slice(g * NBR_GQ, (g + 1) * NBR_GQ)
        for hh in range(NBR_HEADS):
            tiles.append((hh, qrows, start, d0, pattern))

    def scores(tile):
        hh, qrows, start, _, _ = tile
        k = k_refs[hh][0, 0, pl.ds(start, NBR_GK), :]
        return lax.dot_general(q_ref[0, hh, qrows, :], k, (((1,), (1,)), ((), ())), preferred_element_type=F32)

    s_next = scores(tiles[0])
    for i, (hh, qrows, start, d0, pattern) in enumerate(tiles):
        s = s_next
        if i + 1 < len(tiles):
            s_next = scores(tiles[i + 1])
        bias = jnp.concatenate(
            [jnp.concatenate([pair_ref[hh, d0 + 2 * c - rq] for c in range(NBR_KR // 2)], axis=1)
             for rq in range(NBR_QR)], axis=0)
        s = s + bias + mask_ref[pattern]
        m = jnp.max(s, axis=-1, keepdims=True)
        p = jnp.exp2(s - m)
        l = jnp.sum(p, axis=-1, keepdims=True)
        v = v_refs[hh][0, 0, pl.ds(start, NBR_GK), :]
        o = jnp.dot(p.astype(BF16), v, preferred_element_type=F32) / l
        o_ref[0, qrows, hh * HEAD_DIM:(hh + 1) * HEAD_DIM] = o.astype(BF16)


def _nbr(qkv, pair, rowmask, batch, seq):
    rows = seq // GRID_W
    n_off = pair.shape[1]
    hw = NBR_HEADS * HEAD_DIM

    def head_block(cb, hh):
        return pl.BlockSpec((1, 1, seq, HEAD_DIM), lambda b, h, i: (b, cb + NBR_HEADS * h + hh, 0, 0))

    return pl.pallas_call(
        functools.partial(_nbr_kernel, rows=rows),
        grid=(batch, N_HEADS_B // NBR_HEADS, rows // NBR_ROWS),
        in_specs=[pl.BlockSpec((1, NBR_HEADS, NBR_TQ, HEAD_DIM), lambda b, h, i: (b, CB_QB // NBR_HEADS + h, i, 0))]
        + [head_block(CB_KB, hh) for hh in range(NBR_HEADS)]
        + [head_block(CB_VB, hh) for hh in range(NBR_HEADS)]
        + [pl.BlockSpec((NBR_HEADS, n_off, GRID_W, 2 * GRID_W), lambda b, h, i: (h, 0, 0, 0)),
           pl.BlockSpec((3, NBR_GQ, NBR_GK), lambda b, h, i: (0, 0, 0))],
        out_specs=pl.BlockSpec((1, NBR_TQ, hw), lambda b, h, i: (b, i, h)),
        out_shape=jax.ShapeDtypeStruct((batch, seq, WIDTH_B), BF16),
        compiler_params=_params(3),
        name="nbr",
    )(qkv, *([qkv] * (2 * NBR_HEADS)), pair, rowmask)


def _merge_kernel(h_ref, ya_ref, yb_ref, gate_ref, wa_ref, wb_ref, wo_ref, o_ref):
    for r in range(0, MERGE_TM, MERGE_RC):
        rows = slice(r, r + MERGE_RC)
        ta = jnp.dot(ya_ref[rows, :], wa_ref[...], preferred_element_type=F32)
        tb = jnp.dot(yb_ref[rows, :], wb_ref[...], preferred_element_type=F32)
        merged = gate_ref[rows, :D_MODEL] * ta + gate_ref[rows, D_MODEL:] * tb
        o_ref[rows, :] = h_ref[rows, :] + jnp.dot(merged.astype(BF16), wo_ref[...], preferred_element_type=F32)


def _merge(h, ya, yb, gates, wa, wb, wo):
    n = h.shape[0]
    tm = MERGE_TM
    row = lambda i: (i, 0)
    fixed = lambda i: (0, 0)
    return pl.pallas_call(
        _merge_kernel,
        grid=(n // tm,),
        in_specs=[
            pl.BlockSpec((tm, D_MODEL), row),
            pl.BlockSpec((tm, WIDTH_QA), row),
            pl.BlockSpec((tm, WIDTH_B), row),
            pl.BlockSpec((tm, GATE_WIDTH), row),
            pl.BlockSpec((WIDTH_QA, D_MODEL), fixed, pipeline_mode=pl.Buffered(1)),
            pl.BlockSpec((WIDTH_B, D_MODEL), fixed, pipeline_mode=pl.Buffered(1)),
            pl.BlockSpec((D_MODEL, D_MODEL), fixed, pipeline_mode=pl.Buffered(1)),
        ],
        out_specs=pl.BlockSpec((tm, D_MODEL), row),
        out_shape=jax.ShapeDtypeStruct((n, D_MODEL), F32),
        compiler_params=_params(1),
        name="merge_out",
    )(h, ya, yb, gates, wa, wb, wo)


def _rope_tables(seq):
    t = np.arange(seq)
    row = (t // GRID_W).astype(np.float32)
    col = (t % GRID_W).astype(np.float32)
    n_pairs_axis = HEAD_DIM // 4
    inv = (ROPE_THETA ** (-np.arange(n_pairs_axis, dtype=np.float32) / n_pairs_axis)).astype(np.float32)
    ang = np.concatenate([row[:, None] * inv[None], col[:, None] * inv[None]], axis=-1)
    cos = np.cos(ang).astype(np.float32)
    sin = np.sin(ang).astype(np.float32)
    cos2 = np.repeat(cos, 2, axis=-1)
    sin2 = np.stack([-sin, sin], axis=-1).reshape(seq, HEAD_DIM)
    return jnp.asarray(cos2), jnp.asarray(sin2)


def _ffn_weights(w_in, w_out):
    return _cast_cols(w_in, 2, FF_TILE, D_FF_PAD // FF_TILE), _cast_rows(w_out, D_FF_PAD)


def kernel(x_prompt, x_sample, g_ffn1, w_ffn1_in, w_ffn1_out, g_mix, w_in, b_gate, g_q_a, g_k_a, rpb_b,
           w_branch_a, w_branch_b, w_out, g_ffn2, w_ffn2_in, w_ffn2_out, g_final):
    assert g_ffn1.shape[0] == 1, "single-layer stack"
    w1 = _ffn_weights(w_ffn1_in[0], w_ffn1_out[0])
    w2 = _ffn_weights(w_ffn2_in[0], w_ffn2_out[0])
    w_proj = _cast_cols(w_in[0], 1, PROJ_TN, IN_WIDTH // PROJ_TN)
    wa = w_branch_a[0].astype(BF16)
    wb = w_branch_b[0].astype(BF16)
    wo = w_out[0].astype(BF16)
    g1 = g_ffn1[0][None, :]
    gm = g_mix[0][None, :]
    g2 = g_ffn2[0][None, :]
    gf = g_final[None, :]
    gq = g_q_a[0][None, :]
    gk = g_k_a[0][None, :]
    bg = b_gate[0][None, :]
    pair = _nbr_pair_table(rpb_b[0])

    def trunk(x):
        batch, seq, _ = x.shape
        n = batch * seq
        cos2, sin2 = _rope_tables(seq)
        rowmask = _nbr_row_masks(seq // GRID_W)
        h, u = _ffn(x.reshape(n, D_MODEL), g1, *w1, gm, final=False)
        qkv = _proj(u, w_proj, gq, gk, cos2, sin2, batch, seq)
        gates = _gates(u, w_proj, bg, seq)
        ya = _gqa(qkv, batch, seq).reshape(n, WIDTH_QA)
        yb = _nbr(qkv, pair, rowmask, batch, seq).reshape(n, WIDTH_B)
        h2 = _merge(h, ya, yb, gates, wa, wb, wo)
        out = _ffn(h2, g2, *w2, gf, final=True)
        return out.reshape(batch, seq, D_MODEL)

    return (trunk(x_prompt), trunk(x_sample))
```

```python
import functools

import numpy as np
import jax
import jax.numpy as jnp
from jax import lax
from jax.experimental import pallas as pl
from jax.experimental.pallas import tpu as pltpu

D_MODEL = 2048
HEAD_DIM = 128
N_HEADS_A = 8
N_KV_HEADS_A = 2
GROUP_A = N_HEADS_A // N_KV_HEADS_A
N_HEADS_B = 8
GRID_W = 64
WIN_ROWS = 8
WIN_COLS = 16
D_FF = 5504
ROPE_THETA = 10000.0
EPS = 1e-6
NEG = -1e30

WIDTH_QA = N_HEADS_A * HEAD_DIM
WIDTH_KVA = N_KV_HEADS_A * HEAD_DIM
WIDTH_B = N_HEADS_B * HEAD_DIM
QKV_WIDTH = WIDTH_QA + 2 * WIDTH_KVA + 3 * WIDTH_B
GATE_WIDTH = 2 * D_MODEL
IN_WIDTH = QKV_WIDTH + GATE_WIDTH
N_QKV_HEADS = QKV_WIDTH // HEAD_DIM

CB_QA = 0
CB_KA = WIDTH_QA // HEAD_DIM
CB_VA = CB_KA + N_KV_HEADS_A
CB_QB = CB_VA + N_KV_HEADS_A
CB_KB = CB_QB + N_HEADS_B
CB_VB = CB_KB + N_HEADS_B

V7X_VMEM_BYTES = 64 * 1024 * 1024
VMEM_LIMIT = V7X_VMEM_BYTES - 4 * 1024 * 1024

FF_TILE = 512
FF_CHUNKS = -(-D_FF // FF_TILE)
FF_PER_STEP = 1
FF_STEPS = -(-FF_CHUNKS // FF_PER_STEP)
FF_LAST = FF_CHUNKS - FF_PER_STEP * (FF_STEPS - 1)
D_FF_PAD = FF_STEPS * FF_PER_STEP * FF_TILE
FFN_TM = 1024
CAST_ROWS = 128
CAST_COLS = 512
PROJ_TM = 2048
PROJ_TN = 512
PROJ_RC = 512
MERGE_TM = 512
MERGE_RC = 256
GQA_TQ = 512
GQA_STREAMS = 4
GQA_TK = 512
NBR_ROWS = 16
NBR_QR = 4
NBR_GROUPS = NBR_ROWS // NBR_QR
NBR_KR = 12
NBR_HEADS = 2
NBR_TQ = NBR_ROWS * GRID_W
NBR_GQ = NBR_QR * GRID_W
NBR_GK = NBR_KR * GRID_W
NBR_MAX_OFF = NBR_KR - 1

LOG2E = float(np.log2(np.e))
QK_SCALE = HEAD_DIM ** -0.5 * LOG2E

F32 = jnp.float32
BF16 = jnp.bfloat16


def _params(n_axes):
    return pltpu.CompilerParams(dimension_semantics=("arbitrary",) * n_axes,
                                vmem_limit_bytes=VMEM_LIMIT)


def _rms(xf, g):
    ms = jnp.mean(xf * xf, axis=-1, keepdims=True)
    return xf * lax.rsqrt(ms + EPS) * g


def _cast_cols_kernel(w_ref, o_ref, *, valid):
    n_chunks, rows, tn = o_ref.shape
    for c in range(n_chunks):
        lo = c * tn
        width = max(0, min(tn, valid - lo))
        if width > 0:
            o_ref[c, :, :width] = w_ref[:, lo:lo + width].astype(BF16)
        if width < tn:
            o_ref[c, :, width:] = jnp.zeros((rows, tn - width), BF16)


def _cast_cols(w, n_parts, tn, n_chunks):
    r, total = w.shape
    valid = total // n_parts
    assert n_chunks * tn >= valid
    return pl.pallas_call(
        functools.partial(_cast_cols_kernel, valid=valid),
        grid=(n_parts, r // CAST_ROWS),
        in_specs=[pl.BlockSpec((CAST_ROWS, valid), lambda p, i: (i, p))],
        out_specs=pl.BlockSpec((None, n_chunks, CAST_ROWS, tn), lambda p, i: (p, 0, i, 0)),
        out_shape=jax.ShapeDtypeStruct((n_parts, n_chunks, r, tn), BF16),
        compiler_params=_params(2),
        name="cast_cols",
    )(w)


def _cast_rows_kernel(w_ref, o_ref):
    r = w_ref.shape[0]
    o_ref[:r, :] = w_ref[...].astype(BF16)
    o_ref[r:, :] = jnp.zeros((o_ref.shape[0] - r, o_ref.shape[1]), BF16)


def _cast_rows(w, rows_out):
    r, c = w.shape
    return pl.pallas_call(
        _cast_rows_kernel,
        grid=(c // CAST_COLS,),
        in_specs=[pl.BlockSpec((r, CAST_COLS), lambda i: (0, i))],
        out_specs=pl.BlockSpec((rows_out, CAST_COLS), lambda i: (0, i)),
        out_shape=jax.ShapeDtypeStruct((rows_out, c), BF16),
        compiler_params=_params(1),
        name="cast_rows",
    )(w)


def _ffn_kernel(x_ref, g_ref, wg_ref, wu_ref, wo_ref, g2_ref, *refs, final):
    if final:
        acc_ref, xn_ref = refs
    else:
        acc_ref, u_ref, xn_ref = refs
    j = pl.program_id(1)
    last = pl.num_programs(1) - 1

    @pl.when(j == 0)
    def _():
        xn_ref[...] = _rms(x_ref[...], g_ref[...]).astype(BF16)
        acc_ref[...] = jnp.zeros_like(acc_ref)

    def chunk(c):
        xn = xn_ref[...]
        gate = jnp.dot(xn, wg_ref[c], preferred_element_type=F32)
        up = jnp.dot(xn, wu_ref[c], preferred_element_type=F32)
        act = ((gate * jax.nn.sigmoid(gate)) * up).astype(BF16)
        acc_ref[...] += jnp.dot(act, wo_ref[c * FF_TILE:(c + 1) * FF_TILE, :], preferred_element_type=F32)

    @pl.when(j < last)
    def _():
        for c in range(FF_PER_STEP):
            chunk(c)

    @pl.when(j == last)
    def _():
        for c in range(FF_LAST):
            chunk(c)
        h = x_ref[...] + 0.5 * acc_ref[...]
        if final:
            acc_ref[...] = _rms(h, g2_ref[...])
        else:
            acc_ref[...] = h
            u_ref[...] = _rms(h, g2_ref[...]).astype(BF16)


def _ffn(x, g, w_gu, wo, g2, *, final):
    n = x.shape[0]
    tm = FFN_TM
    row = lambda i, j: (i, 0)
    in_specs = [
        pl.BlockSpec((tm, D_MODEL), row),
        pl.BlockSpec((1, D_MODEL), lambda i, j: (0, 0)),
        pl.BlockSpec((None, FF_PER_STEP, D_MODEL, FF_TILE), lambda i, j: (0, j, 0, 0)),
        pl.BlockSpec((None, FF_PER_STEP, D_MODEL, FF_TILE), lambda i, j: (1, j, 0, 0)),
        pl.BlockSpec((FF_PER_STEP * FF_TILE, D_MODEL), lambda i, j: (j, 0)),
        pl.BlockSpec((1, D_MODEL), lambda i, j: (0, 0)),
    ]
    if final:
        out_shape = jax.ShapeDtypeStruct((n, D_MODEL), F32)
        out_specs = pl.BlockSpec((tm, D_MODEL), row)
    else:
        out_shape = (jax.ShapeDtypeStruct((n, D_MODEL), F32), jax.ShapeDtypeStruct((n, D_MODEL), BF16))
        out_specs = (pl.BlockSpec((tm, D_MODEL), row),
                     pl.BlockSpec((tm, D_MODEL), row, pipeline_mode=pl.Buffered(1)))
    return pl.pallas_call(
        functools.partial(_ffn_kernel, final=final),
        grid=(n // tm, FF_STEPS),
        in_specs=in_specs,
        out_specs=out_specs,
        out_shape=out_shape,
        scratch_shapes=[pltpu.VMEM((tm, D_MODEL), BF16)],
        compiler_params=_params(2),
        name="ffn_final" if final else "ffn_first",
    )(x, g, w_gu, w_gu, wo, g2)


def _rope(y, cos2, sin2):
    lane = lax.broadcasted_iota(jnp.int32, y.shape, 1)
    partner = jnp.where(lane % 2 == 0, pltpu.roll(y, HEAD_DIM - 1, 1), pltpu.roll(y, 1, 1))
    return y * cos2 + partner * sin2


def _row_chunks(n_rows):
    return [slice(r, r + PROJ_RC) for r in range(0, n_rows, PROJ_RC)]


def _proj_kernel(u_ref, w_ref, gq_ref, gk_ref, cos_ref, sin_ref, o_ref):
    j = pl.program_id(1)
    heads_per_tile = PROJ_TN // HEAD_DIM
    qa_tiles = WIDTH_QA // PROJ_TN
    kva_tile = qa_tiles
    qb_lo = (CB_QB * HEAD_DIM) // PROJ_TN
    qb_hi = (CB_KB * HEAD_DIM) // PROJ_TN

    def normed_rope(yh, g_ref, rows):
        return _rope(_rms(yh, g_ref[...]), cos_ref[rows, :], sin_ref[rows, :])

    def for_heads(epilogue):
        for rows in _row_chunks(u_ref.shape[0]):
            y = jnp.dot(u_ref[rows, :], w_ref[...], preferred_element_type=F32)
            for hh in range(heads_per_tile):
                o_ref[0, hh, rows, :] = epilogue(hh, rows, y[:, hh * HEAD_DIM:(hh + 1) * HEAD_DIM]).astype(BF16)

    @pl.when(j < qa_tiles)
    def _():
        for_heads(lambda hh, rows, yh: normed_rope(yh, gq_ref, rows) * QK_SCALE)

    @pl.when(j == kva_tile)
    def _():
        for_heads(lambda hh, rows, yh: normed_rope(yh, gk_ref, rows) if hh < N_KV_HEADS_A else yh)

    @pl.when((j >= qb_lo) & (j < qb_hi))
    def _():
        for_heads(lambda hh, rows, yh: yh * QK_SCALE)

    @pl.when(j >= qb_hi)
    def _():
        for_heads(lambda hh, rows, yh: yh)


def _proj(u, w_blocks, gq, gk, cos2, sin2, batch, seq):
    n = u.shape[0]
    tm = min(PROJ_TM, seq)
    pos_tiles = seq // tm
    heads_per_tile = PROJ_TN // HEAD_DIM
    return pl.pallas_call(
        _proj_kernel,
        grid=(n // tm, QKV_WIDTH // PROJ_TN),
        in_specs=[
            pl.BlockSpec((tm, D_MODEL), lambda i, j: (i, 0)),
            pl.BlockSpec((None, None, D_MODEL, PROJ_TN), lambda i, j: (0, j, 0, 0)),
            pl.BlockSpec((1, HEAD_DIM), lambda i, j: (0, 0)),
            pl.BlockSpec((1, HEAD_DIM), lambda i, j: (0, 0)),
            pl.BlockSpec((tm, HEAD_DIM), lambda i, j: (i % pos_tiles, 0)),
            pl.BlockSpec((tm, HEAD_DIM), lambda i, j: (i % pos_tiles, 0)),
        ],
        out_specs=pl.BlockSpec((1, heads_per_tile, tm, HEAD_DIM),
                               lambda i, j: (i // pos_tiles, j, i % pos_tiles, 0)),
        out_shape=jax.ShapeDtypeStruct((batch, N_QKV_HEADS, seq, HEAD_DIM), BF16),
        compiler_params=_params(2),
        name="proj_qkv",
    )(u, w_blocks, gq, gk, cos2, sin2)


def _gate_kernel(u_ref, w_ref, b_ref, o_ref):
    for rows in _row_chunks(u_ref.shape[0]):
        y = jnp.dot(u_ref[rows, :], w_ref[...], preferred_element_type=F32)
        o_ref[rows, :] = jax.nn.sigmoid(y + b_ref[...])


def _gates(u, w_blocks, b_gate, seq):
    n = u.shape[0]
    tm = min(PROJ_TM, seq)
    first = QKV_WIDTH // PROJ_TN
    return pl.pallas_call(
        _gate_kernel,
        grid=(n // tm, GATE_WIDTH // PROJ_TN),
        in_specs=[
            pl.BlockSpec((tm, D_MODEL), lambda i, j: (i, 0)),
            pl.BlockSpec((None, None, D_MODEL, PROJ_TN), lambda i, j: (0, first + j, 0, 0)),
            pl.BlockSpec((1, PROJ_TN), lambda i, j: (0, j)),
        ],
        out_specs=pl.BlockSpec((tm, PROJ_TN), lambda i, j: (i, j)),
        out_shape=jax.ShapeDtypeStruct((n, GATE_WIDTH), F32),
        compiler_params=_params(2),
        name="proj_gates",
    )(u, w_blocks, b_gate)


def _gqa_kernel(q_ref, k_ref, v_ref, o_ref, *, seq):
    ts = GQA_TQ // GQA_STREAMS
    rows = GROUP_A * ts
    qs, ms, ls, accs = [], [], [], []
    for t in range(GQA_STREAMS):
        tok = slice(t * ts, (t + 1) * ts)
        qs.append(jnp.concatenate([q_ref[0, g, tok, :] for g in range(GROUP_A)], axis=0))
        ms.append(jnp.full((1, rows), -jnp.inf, F32))
        ls.append(jnp.zeros((1, rows), F32))
        accs.append(jnp.zeros((HEAD_DIM, rows), F32))
    n_chunks = seq // GQA_TK

    def scores(c, t):
        k = k_ref[0, 0, c * GQA_TK:(c + 1) * GQA_TK, :]
        return lax.dot_general(k, qs[t], (((1,), (1,)), ((), ())), preferred_element_type=F32)

    work = [(c, t) for c in range(n_chunks) for t in range(GQA_STREAMS)]
    st_next = scores(*work[0])
    for i, (c, t) in enumerate(work):
        st = st_next
        if i + 1 < len(work):
            st_next = scores(*work[i + 1])
        v = v_ref[0, 0, c * GQA_TK:(c + 1) * GQA_TK, :]
        m_new = jnp.maximum(ms[t], jnp.max(st, axis=0, keepdims=True))
        alpha = jnp.exp2(ms[t] - m_new)
        p = jnp.exp2(st - m_new)
        ls[t] = alpha * ls[t] + jnp.sum(p, axis=0, keepdims=True)
        pv = lax.dot_general(v, p.astype(BF16), (((0,), (0,)), ((), ())), preferred_element_type=F32)
        accs[t] = alpha * accs[t] + pv
        ms[t] = m_new
    for t in range(GQA_STREAMS):
        o = (accs[t] / ls[t]).T
        for g in range(GROUP_A):
            o_ref[0, t * ts:(t + 1) * ts, g * HEAD_DIM:(g + 1) * HEAD_DIM] = o[g * ts:(g + 1) * ts].astype(BF16)


def _gqa(qkv, batch, seq):
    gw = GROUP_A * HEAD_DIM
    return pl.pallas_call(
        functools.partial(_gqa_kernel, seq=seq),
        grid=(batch, N_KV_HEADS_A, seq // GQA_TQ),
        in_specs=[
            pl.BlockSpec((1, GROUP_A, GQA_TQ, HEAD_DIM), lambda b, h, i: (b, h, i, 0)),
            pl.BlockSpec((1, 1, seq, HEAD_DIM), lambda b, h, i: (b, CB_KA + h, 0, 0)),
            pl.BlockSpec((1, 1, seq, HEAD_DIM), lambda b, h, i: (b, CB_VA + h, 0, 0)),
        ],
        out_specs=pl.BlockSpec((1, GQA_TQ, gw), lambda b, h, i: (b, i, h)),
        out_shape=jax.ShapeDtypeStruct((batch, seq, WIDTH_QA), BF16),
        compiler_params=_params(3),
        name="gqa",
    )(qkv, qkv, qkv)


def _nbr_pair_table(rpb):
    qc = np.arange(GRID_W)[:, None]
    kc = np.arange(GRID_W)[None, :]
    cs = np.clip(qc - WIN_COLS // 2, 0, GRID_W - WIN_COLS)
    col_ok = (kc >= cs) & (kc < cs + WIN_COLS)
    dc = np.clip(kc - qc + WIN_COLS - 1, 0, 2 * WIN_COLS - 2)
    onehot = (dc.reshape(1, -1) == np.arange(2 * WIN_COLS - 1)[:, None]).astype(np.float32)
    n_dr = 2 * WIN_ROWS - 1
    picked = jnp.dot(rpb.astype(F32).reshape(N_HEADS_B * n_dr, -1), jnp.asarray(onehot),
                     precision=lax.Precision.HIGHEST).reshape(N_HEADS_B, n_dr, GRID_W, GRID_W)
    per_dr = jnp.where(col_ok[None, None], picked * LOG2E, NEG)
    pad = NBR_MAX_OFF - (WIN_ROWS - 1)
    blocked = jnp.full((N_HEADS_B, pad, GRID_W, GRID_W), NEG, F32)
    ext = jnp.concatenate([blocked, per_dr, blocked], axis=1)
    return jnp.concatenate([ext[:, :-1], ext[:, 1:]], axis=-1)


def _nbr_row_masks(rows):
    n_groups = rows // NBR_QR
    assert n_groups >= 3
    masks = []
    for gg in (0, 1, n_groups - 1):
        r = gg * NBR_QR + np.arange(NBR_QR)
        ws = int(np.clip(gg * NBR_QR - WIN_ROWS // 2, 0, rows - NBR_KR))
        rs = np.clip(r - WIN_ROWS // 2, 0, rows - WIN_ROWS)
        rk = ws + np.arange(NBR_KR)
        ok = (rk[None, :] >= rs[:, None]) & (rk[None, :] < rs[:, None] + WIN_ROWS)
        assert ok.sum(axis=1).min() == WIN_ROWS
        m = np.where(ok, 0.0, NEG).astype(np.float32)
        masks.append(np.repeat(np.repeat(m, GRID_W, axis=0), GRID_W, axis=1))
    return jnp.asarray(np.stack(masks))


def _nbr_kernel(q_ref, *refs, rows):
    k_refs = refs[:NBR_HEADS]
    v_refs = refs[NBR_HEADS:2 * NBR_HEADS]
    pair_ref, mask_ref, o_ref = refs[2 * NBR_HEADS:]
    rb = pl.program_id(2)
    n_groups = rows // NBR_QR
    tiles = []
    for g in range(NBR_GROUPS):
        gg = rb * NBR_GROUPS + g
        ws = jnp.clip(gg * NBR_QR - WIN_ROWS // 2, 0, rows - NBR_KR)
        pattern = jnp.where(gg == 0, 0, jnp.where(gg == n_groups - 1, 2, 1))
        d0 = ws - gg * NBR_QR + NBR_MAX_OFF
        start = pl.multiple_of(ws * GRID_W, GRID_W)
        qrows = slice(g * NBR_GQ, (g + 1) * NBR_GQ)
        for hh in range(NBR_HEADS):
            tiles.append((hh, qrows, start, d0, pattern))

    def scores(tile):
        hh, qrows, start, _, _ = tile
        k = k_refs[hh][0, 0, pl.ds(start, NBR_GK), :]
        return lax.dot_general(q_ref[0, hh, qrows, :], k, (((1,), (1,)), ((), ())), preferred_element_type=F32)

    s_next = scores(tiles[0])
    for i, (hh, qrows, start, d0, pattern) in enumerate(tiles):
        s = s_next
        if i + 1 < len(tiles):
            s_next = scores(tiles[i + 1])
        bias = jnp.concatenate(
            [jnp.concatenate([pair_ref[hh, d0 + 2 * c - rq] for c in range(NBR_KR // 2)], axis=1)
             for rq in range(NBR_QR)], axis=0)
        s = s + bias + mask_ref[pattern]
        m = jnp.max(s, axis=-1, keepdims=True)
        p = jnp.exp2(s - m)
        l = jnp.sum(p, axis=-1, keepdims=True)
        v = v_refs[hh][0, 0, pl.ds(start, NBR_GK), :]
        o = jnp.dot(p.astype(BF16), v, preferred_element_type=F32) / l
        o_ref[0, qrows, hh * HEAD_DIM:(hh + 1) * HEAD_DIM] = o.astype(BF16)


def _nbr(qkv, pair, rowmask, batch, seq):
    rows = seq // GRID_W
    n_off = pair.shape[1]
    hw = NBR_HEADS * HEAD_DIM

    def head_block(cb, hh):
        return pl.BlockSpec((1, 1, seq, HEAD_DIM), lambda b, h, i: (b, cb + NBR_HEADS * h + hh, 0, 0))

    return pl.pallas_call(
        functools.partial(_nbr_kernel, rows=rows),
        grid=(batch, N_HEADS_B // NBR_HEADS, rows // NBR_ROWS),
        in_specs=[pl.BlockSpec((1, NBR_HEADS, NBR_TQ, HEAD_DIM), lambda b, h, i: (b, CB_QB // NBR_HEADS + h, i, 0))]
        + [head_block(CB_KB, hh) for hh in range(NBR_HEADS)]
        + [head_block(CB_VB, hh) for hh in range(NBR_HEADS)]
        + [pl.BlockSpec((NBR_HEADS, n_off, GRID_W, 2 * GRID_W), lambda b, h, i: (h, 0, 0, 0)),
           pl.BlockSpec((3, NBR_GQ, NBR_GK), lambda b, h, i: (0, 0, 0))],
        out_specs=pl.BlockSpec((1, NBR_TQ, hw), lambda b, h, i: (b, i, h)),
        out_shape=jax.ShapeDtypeStruct((batch, seq, WIDTH_B), BF16),
        compiler_params=_params(3),
        name="nbr",
    )(qkv, *([qkv] * (2 * NBR_HEADS)), pair, rowmask)


def _merge_kernel(h_ref, ya_ref, yb_ref, gate_ref, wa_ref, wb_ref, wo_ref, o_ref):
    for r in range(0, MERGE_TM, MERGE_RC):
        rows = slice(r, r + MERGE_RC)
        ta = jnp.dot(ya_ref[rows, :], wa_ref[...], preferred_element_type=F32)
        tb = jnp.dot(yb_ref[rows, :], wb_ref[...], preferred_element_type=F32)
        merged = gate_ref[rows, :D_MODEL] * ta + gate_ref[rows, D_MODEL:] * tb
        o_ref[rows, :] = h_ref[rows, :] + jnp.dot(merged.astype(BF16), wo_ref[...], preferred_element_type=F32)


def _merge(h, ya, yb, gates, wa, wb, wo):
    n = h.shape[0]
    tm = MERGE_TM
    row = lambda i: (i, 0)
    fixed = lambda i: (0, 0)
    return pl.pallas_call(
        _merge_kernel,
        grid=(n // tm,),
        in_specs=[
            pl.BlockSpec((tm, D_MODEL), row),
            pl.BlockSpec((tm, WIDTH_QA), row),
            pl.BlockSpec((tm, WIDTH_B), row),
            pl.BlockSpec((tm, GATE_WIDTH), row),
            pl.BlockSpec((WIDTH_QA, D_MODEL), fixed, pipeline_mode=pl.Buffered(1)),
            pl.BlockSpec((WIDTH_B, D_MODEL), fixed, pipeline_mode=pl.Buffered(1)),
            pl.BlockSpec((D_MODEL, D_MODEL), fixed, pipeline_mode=pl.Buffered(1)),
        ],
        out_specs=pl.BlockSpec((tm, D_MODEL), row),
        out_shape=jax.ShapeDtypeStruct((n, D_MODEL), F32),
        compiler_params=_params(1),
        name="merge_out",
    )(h, ya, yb, gates, wa, wb, wo)


def _rope_tables(seq):
    t = np.arange(seq)
    row = (t // GRID_W).astype(np.float32)
    col = (t % GRID_W).astype(np.float32)
    n_pairs_axis = HEAD_DIM // 4
    inv = (ROPE_THETA ** (-np.arange(n_pairs_axis, dtype=np.float32) / n_pairs_axis)).astype(np.float32)
    ang = np.concatenate([row[:, None] * inv[None], col[:, None] * inv[None]], axis=-1)
    cos = np.cos(ang).astype(np.float32)
    sin = np.sin(ang).astype(np.float32)
    cos2 = np.repeat(cos, 2, axis=-1)
    sin2 = np.stack([-sin, sin], axis=-1).reshape(seq, HEAD_DIM)
    return jnp.asarray(cos2), jnp.asarray(sin2)


def _ffn_weights(w_in, w_out):
    return _cast_cols(w_in, 2, FF_TILE, D_FF_PAD // FF_TILE), _cast_rows(w_out, D_FF_PAD)


def kernel(x_prompt, x_sample, g_ffn1, w_ffn1_in, w_ffn1_out, g_mix, w_in, b_gate, g_q_a, g_k_a, rpb_b,
           w_branch_a, w_branch_b, w_out, g_ffn2, w_ffn2_in, w_ffn2_out, g_final):
    assert g_ffn1.shape[0] == 1, "single-layer stack"
    w1 = _ffn_weights(w_ffn1_in[0], w_ffn1_out[0])
    w2 = _ffn_weights(w_ffn2_in[0], w_ffn2_out[0])
    w_proj = _cast_cols(w_in[0], 1, PROJ_TN, IN_WIDTH // PROJ_TN)
    wa = w_branch_a[0].astype(BF16)
    wb = w_branch_b[0].astype(BF16)
    wo = w_out[0].astype(BF16)
    g1 = g_ffn1[0][None, :]
    gm = g_mix[0][None, :]
    g2 = g_ffn2[0][None, :]
    gf = g_final[None, :]
    gq = g_q_a[0][None, :]
    gk = g_k_a[0][None, :]
    bg = b_gate[0][None, :]
    pair = _nbr_pair_table(rpb_b[0])

    def trunk(x):
        batch, seq, _ = x.shape
        n = batch * seq
        cos2, sin2 = _rope_tables(seq)
        rowmask = _nbr_row_masks(seq // GRID_W)
        h, u = _ffn(x.reshape(n, D_MODEL), g1, *w1, gm, final=False)
        qkv = _proj(u, w_proj, gq, gk, cos2, sin2, batch, seq)
        gates = _gates(u, w_proj, bg, seq)
        ya = _gqa(qkv, batch, seq).reshape(n, WIDTH_QA)
        yb = _nbr(qkv, pair, rowmask, batch, seq).reshape(n, WIDTH_B)
        h2 = _merge(h, ya, yb, gates, wa, wb, wo)
        out = _ffn(h2, g2, *w2, gf, final=True)
        return out.reshape(batch, seq, D_MODEL)

    return (trunk(x_prompt), trunk(x_sample))
```

```python
import functools

import numpy as np
import jax
import jax.numpy as jnp
from jax import lax
from jax.experimental import pallas as pl
from jax.experimental.pallas import tpu as pltpu

D_MODEL = 2048
HEAD_DIM = 128
N_HEADS_A = 8
N_KV_HEADS_A = 2
GROUP_A = N_HEADS_A // N_KV_HEADS_A
N_HEADS_B = 8
GRID_W = 64
WIN_ROWS = 8
WIN_COLS = 16
D_FF = 5504
ROPE_THETA = 10000.0
EPS = 1e-6
NEG = -1e30

WIDTH_QA = N_HEADS_A * HEAD_DIM
WIDTH_KVA = N_KV_HEADS_A * HEAD_DIM
WIDTH_B = N_HEADS_B * HEAD_DIM
QKV_WIDTH = WIDTH_QA + 2 * WIDTH_KVA + 3 * WIDTH_B
GATE_WIDTH = 2 * D_MODEL
IN_WIDTH = QKV_WIDTH + GATE_WIDTH
N_QKV_HEADS = QKV_WIDTH // HEAD_DIM

CB_QA = 0
CB_KA = WIDTH_QA // HEAD_DIM
CB_VA = CB_KA + N_KV_HEADS_A
CB_QB = CB_VA + N_KV_HEADS_A
CB_KB = CB_QB + N_HEADS_B
CB_VB = CB_KB + N_HEADS_B

V7X_VMEM_BYTES = 64 * 1024 * 1024
VMEM_LIMIT = V7X_VMEM_BYTES - 2 * 1024 * 1024

FF_TILE = 512
FF_CHUNKS = -(-D_FF // FF_TILE)
FF_PER_STEP = 1
FF_STEPS = -(-FF_CHUNKS // FF_PER_STEP)
FF_LAST = FF_CHUNKS - FF_PER_STEP * (FF_STEPS - 1)
D_FF_PAD = FF_STEPS * FF_PER_STEP * FF_TILE
FFN_TM = 1024
CAST_ROWS = 128
CAST_COLS = 512
PROJ_TM = 2048
PROJ_TN = 512
PROJ_RC = 512
MERGE_TM = 512
MERGE_RC = 256
GQA_TQ = 512
GQA_STREAMS = 4
GQA_TK = 512
NBR_ROWS = 16
NBR_QR = 4
NBR_GROUPS = NBR_ROWS // NBR_QR
NBR_KR = 12
NBR_HEADS = 2
NBR_TQ = NBR_ROWS * GRID_W
NBR_GQ = NBR_QR * GRID_W
NBR_GK = NBR_KR * GRID_W
NBR_MAX_OFF = NBR_KR - 1

LOG2E = float(np.log2(np.e))
QK_SCALE = HEAD_DIM ** -0.5 * LOG2E

F32 = jnp.float32
BF16 = jnp.bfloat16


def _params(n_axes):
    return pltpu.CompilerParams(dimension_semantics=("arbitrary",) * n_axes,
                                vmem_limit_bytes=VMEM_LIMIT)


def _rms(xf, g):
    ms = jnp.mean(xf * xf, axis=-1, keepdims=True)
    return xf * lax.rsqrt(ms + EPS) * g


def _cast_cols_kernel(w_ref, o_ref, *, valid):
    n_chunks, rows, tn = o_ref.shape
    for c in range(n_chunks):
        lo = c * tn
        width = max(0, min(tn, valid - lo))
        if width > 0:
            o_ref[c, :, :width] = w_ref[:, lo:lo + width].astype(BF16)
        if width < tn:
            o_ref[c, :, width:] = jnp.zeros((rows, tn - width), BF16)


def _cast_cols(w, n_parts, tn, n_chunks):
    r, total = w.shape
    valid = total // n_parts
    assert n_chunks * tn >= valid
    return pl.pallas_call(
        functools.partial(_cast_cols_kernel, valid=valid),
        grid=(n_parts, r // CAST_ROWS),
        in_specs=[pl.BlockSpec((CAST_ROWS, valid), lambda p, i: (i, p))],
        out_specs=pl.BlockSpec((None, n_chunks, CAST_ROWS, tn), lambda p, i: (p, 0, i, 0)),
        out_shape=jax.ShapeDtypeStruct((n_parts, n_chunks, r, tn), BF16),
        compiler_params=_params(2),
        name="cast_cols",
    )(w)


def _cast_rows_kernel(w_ref, o_ref):
    r = w_ref.shape[0]
    o_ref[:r, :] = w_ref[...].astype(BF16)
    o_ref[r:, :] = jnp.zeros((o_ref.shape[0] - r, o_ref.shape[1]), BF16)


def _cast_rows(w, rows_out):
    r, c = w.shape
    return pl.pallas_call(
        _cast_rows_kernel,
        grid=(c // CAST_COLS,),
        in_specs=[pl.BlockSpec((r, CAST_COLS), lambda i: (0, i))],
        out_specs=pl.BlockSpec((rows_out, CAST_COLS), lambda i: (0, i)),
        out_shape=jax.ShapeDtypeStruct((rows_out, c), BF16),
        compiler_params=_params(1),
        name="cast_rows",
    )(w)


def _ffn_kernel(x_ref, g_ref, wg_ref, wu_ref, wo_ref, g2_ref, *refs, final):
    if final:
        acc_ref, xn_ref = refs
    else:
        acc_ref, u_ref, xn_ref = refs
    j = pl.program_id(1)
    last = pl.num_programs(1) - 1

    @pl.when(j == 0)
    def _():
        xn_ref[...] = _rms(x_ref[...], g_ref[...]).astype(BF16)
        acc_ref[...] = jnp.zeros_like(acc_ref)

    def chunk(c):
        xn = xn_ref[...]
        gate = jnp.dot(xn, wg_ref[c], preferred_element_type=F32)
        up = jnp.dot(xn, wu_ref[c], preferred_element_type=F32)
        act = ((gate * jax.nn.sigmoid(gate)) * up).astype(BF16)
        acc_ref[...] += jnp.dot(act, wo_ref[c * FF_TILE:(c + 1) * FF_TILE, :], preferred_element_type=F32)

    @pl.when(j < last)
    def _():
        for c in range(FF_PER_STEP):
            chunk(c)

    @pl.when(j == last)
    def _():
        for c in range(FF_LAST):
            chunk(c)
        h = x_ref[...] + 0.5 * acc_ref[...]
        if final:
            acc_ref[...] = _rms(h, g2_ref[...])
        else:
            acc_ref[...] = h
            u_ref[...] = _rms(h, g2_ref[...]).astype(BF16)


def _ffn(x, g, w_gu, wo, g2, *, final):
    n = x.shape[0]
    tm = FFN_TM
    row = lambda i, j: (i, 0)
    in_specs = [
        pl.BlockSpec((tm, D_MODEL), row),
        pl.BlockSpec((1, D_MODEL), lambda i, j: (0, 0)),
        pl.BlockSpec((None, FF_PER_STEP, D_MODEL, FF_TILE), lambda i, j: (0, j, 0, 0)),
        pl.BlockSpec((None, FF_PER_STEP, D_MODEL, FF_TILE), lambda i, j: (1, j, 0, 0)),
        pl.BlockSpec((FF_PER_STEP * FF_TILE, D_MODEL), lambda i, j: (j, 0)),
        pl.BlockSpec((1, D_MODEL), lambda i, j: (0, 0)),
    ]
    if final:
        out_shape = jax.ShapeDtypeStruct((n, D_MODEL), F32)
        out_specs = pl.BlockSpec((tm, D_MODEL), row)
    else:
        out_shape = (jax.ShapeDtypeStruct((n, D_MODEL), F32), jax.ShapeDtypeStruct((n, D_MODEL), BF16))
        out_specs = (pl.BlockSpec((tm, D_MODEL), row), pl.BlockSpec((tm, D_MODEL), row))
    return pl.pallas_call(
        functools.partial(_ffn_kernel, final=final),
        grid=(n // tm, FF_STEPS),
        in_specs=in_specs,
        out_specs=out_specs,
        out_shape=out_shape,
        scratch_shapes=[pltpu.VMEM((tm, D_MODEL), BF16)],
        compiler_params=_params(2),
        name="ffn_final" if final else "ffn_first",
    )(x, g, w_gu, w_gu, wo, g2)


def _rope(y, cos2, sin2):
    lane = lax.broadcasted_iota(jnp.int32, y.shape, 1)
    partner = jnp.where(lane % 2 == 0, pltpu.roll(y, HEAD_DIM - 1, 1), pltpu.roll(y, 1, 1))
    return y * cos2 + partner * sin2


def _row_chunks(n_rows):
    return [slice(r, r + PROJ_RC) for r in range(0, n_rows, PROJ_RC)]


def _proj_kernel(u_ref, w_ref, gq_ref, gk_ref, cos_ref, sin_ref, o_ref):
    j = pl.program_id(1)
    heads_per_tile = PROJ_TN // HEAD_DIM
    qa_tiles = WIDTH_QA // PROJ_TN
    kva_tile = qa_tiles
    qb_lo = (CB_QB * HEAD_DIM) // PROJ_TN
    qb_hi = (CB_KB * HEAD_DIM) // PROJ_TN

    def normed_rope(yh, g_ref, rows):
        return _rope(_rms(yh, g_ref[...]), cos_ref[rows, :], sin_ref[rows, :])

    def for_heads(epilogue):
        for rows in _row_chunks(u_ref.shape[0]):
            y = jnp.dot(u_ref[rows, :], w_ref[...], preferred_element_type=F32)
            for hh in range(heads_per_tile):
                o_ref[0, hh, rows, :] = epilogue(hh, rows, y[:, hh * HEAD_DIM:(hh + 1) * HEAD_DIM]).astype(BF16)

    @pl.when(j < qa_tiles)
    def _():
        for_heads(lambda hh, rows, yh: normed_rope(yh, gq_ref, rows) * QK_SCALE)

    @pl.when(j == kva_tile)
    def _():
        for_heads(lambda hh, rows, yh: normed_rope(yh, gk_ref, rows) if hh < N_KV_HEADS_A else yh)

    @pl.when((j >= qb_lo) & (j < qb_hi))
    def _():
        for_heads(lambda hh, rows, yh: yh * QK_SCALE)

    @pl.when(j >= qb_hi)
    def _():
        for_heads(lambda hh, rows, yh: yh)


def _proj(u, w_blocks, gq, gk, cos2, sin2, batch, seq):
    n = u.shape[0]
    tm = min(PROJ_TM, seq)
    pos_tiles = seq // tm
    heads_per_tile = PROJ_TN // HEAD_DIM
    return pl.pallas_call(
        _proj_kernel,
        grid=(n // tm, QKV_WIDTH // PROJ_TN),
        in_specs=[
            pl.BlockSpec((tm, D_MODEL), lambda i, j: (i, 0)),
            pl.BlockSpec((None, None, D_MODEL, PROJ_TN), lambda i, j: (0, j, 0, 0)),
            pl.BlockSpec((1, HEAD_DIM), lambda i, j: (0, 0)),
            pl.BlockSpec((1, HEAD_DIM), lambda i, j: (0, 0)),
            pl.BlockSpec((tm, HEAD_DIM), lambda i, j: (i % pos_tiles, 0)),
            pl.BlockSpec((tm, HEAD_DIM), lambda i, j: (i % pos_tiles, 0)),
        ],
        out_specs=pl.BlockSpec((1, heads_per_tile, tm, HEAD_DIM),
                               lambda i, j: (i // pos_tiles, j, i % pos_tiles, 0)),
        out_shape=jax.ShapeDtypeStruct((batch, N_QKV_HEADS, seq, HEAD_DIM), BF16),
        compiler_params=_params(2),
        name="proj_qkv",
    )(u, w_blocks, gq, gk, cos2, sin2)


def _gate_kernel(u_ref, w_ref, b_ref, o_ref):
    for rows in _row_chunks(u_ref.shape[0]):
        y = jnp.dot(u_ref[rows, :], w_ref[...], preferred_element_type=F32)
        o_ref[rows, :] = 0.5 * jnp.tanh(0.5 * (y + b_ref[...])) + 0.5


def _gates(u, w_blocks, b_gate, seq):
    n = u.shape[0]
    tm = min(PROJ_TM, seq)
    first = QKV_WIDTH // PROJ_TN
    return pl.pallas_call(
        _gate_kernel,
        grid=(n // tm, GATE_WIDTH // PROJ_TN),
        in_specs=[
            pl.BlockSpec((tm, D_MODEL), lambda i, j: (i, 0)),
            pl.BlockSpec((None, None, D_MODEL, PROJ_TN), lambda i, j: (0, first + j, 0, 0)),
            pl.BlockSpec((1, PROJ_TN), lambda i, j: (0, j)),
        ],
        out_specs=pl.BlockSpec((tm, PROJ_TN), lambda i, j: (i, j)),
        out_shape=jax.ShapeDtypeStruct((n, GATE_WIDTH), F32),
        compiler_params=_params(2),
        name="proj_gates",
    )(u, w_blocks, b_gate)


def _gqa_kernel(q_ref, k_ref, v_ref, o_ref, *, seq):
    ts = GQA_TQ // GQA_STREAMS
    rows = GROUP_A * ts
    qs, ms, ls, accs = [], [], [], []
    for t in range(GQA_STREAMS):
        tok = slice(t * ts, (t + 1) * ts)
        q = jnp.concatenate([q_ref[0, g, tok, :] for g in range(GROUP_A)], axis=0)
        qs.append(q.T)
        ms.append(jnp.full((1, rows), -jnp.inf, F32))
        ls.append(jnp.zeros((1, rows), F32))
        accs.append(jnp.zeros((HEAD_DIM, rows), F32))
    n_chunks = seq // GQA_TK

    def scores(c, t):
        k = k_ref[0, 0, c * GQA_TK:(c + 1) * GQA_TK, :]
        return jnp.dot(k, qs[t], preferred_element_type=F32)

    work = [(c, t) for c in range(n_chunks) for t in range(GQA_STREAMS)]
    st_next = scores(*work[0])
    for i, (c, t) in enumerate(work):
        st = st_next
        if i + 1 < len(work):
            st_next = scores(*work[i + 1])
        v = v_ref[0, 0, c * GQA_TK:(c + 1) * GQA_TK, :]
        m_new = jnp.maximum(ms[t], jnp.max(st, axis=0, keepdims=True))
        alpha = jnp.exp2(ms[t] - m_new)
        p = jnp.exp2(st - m_new)
        ls[t] = alpha * ls[t] + jnp.sum(p, axis=0, keepdims=True)
        pv = lax.dot_general(v, p.astype(BF16), (((0,), (0,)), ((), ())), preferred_element_type=F32)
        accs[t] = alpha * accs[t] + pv
        ms[t] = m_new
    for t in range(GQA_STREAMS):
        o = (accs[t] / ls[t]).T
        for g in range(GROUP_A):
            o_ref[0, t * ts:(t + 1) * ts, g * HEAD_DIM:(g + 1) * HEAD_DIM] = o[g * ts:(g + 1) * ts].astype(BF16)


def _gqa(qkv, batch, seq):
    gw = GROUP_A * HEAD_DIM
    return pl.pallas_call(
        functools.partial(_gqa_kernel, seq=seq),
        grid=(batch, N_KV_HEADS_A, seq // GQA_TQ),
        in_specs=[
            pl.BlockSpec((1, GROUP_A, GQA_TQ, HEAD_DIM), lambda b, h, i: (b, h, i, 0)),
            pl.BlockSpec((1, 1, seq, HEAD_DIM), lambda b, h, i: (b, CB_KA + h, 0, 0)),
            pl.BlockSpec((1, 1, seq, HEAD_DIM), lambda b, h, i: (b, CB_VA + h, 0, 0)),
        ],
        out_specs=pl.BlockSpec((1, GQA_TQ, gw), lambda b, h, i: (b, i, h)),
        out_shape=jax.ShapeDtypeStruct((batch, seq, WIDTH_QA), BF16),
        compiler_params=_params(3),
        name="gqa",
    )(qkv, qkv, qkv)


def _nbr_pair_table(rpb):
    qc = np.arange(GRID_W)[:, None]
    kc = np.arange(GRID_W)[None, :]
    cs = np.clip(qc - WIN_COLS // 2, 0, GRID_W - WIN_COLS)
    col_ok = (kc >= cs) & (kc < cs + WIN_COLS)
    dc = np.clip(kc - qc + WIN_COLS - 1, 0, 2 * WIN_COLS - 2)
    onehot = (dc.reshape(1, -1) == np.arange(2 * WIN_COLS - 1)[:, None]).astype(np.float32)
    n_dr = 2 * WIN_ROWS - 1
    picked = jnp.dot(rpb.astype(F32).reshape(N_HEADS_B * n_dr, -1), jnp.asarray(onehot),
                     precision=lax.Precision.HIGHEST).reshape(N_HEADS_B, n_dr, GRID_W, GRID_W)
    per_dr = jnp.where(col_ok[None, None], picked * LOG2E, NEG)
    pad = NBR_MAX_OFF - (WIN_ROWS - 1)
    blocked = jnp.full((N_HEADS_B, pad, GRID_W, GRID_W), NEG, F32)
    ext = jnp.concatenate([blocked, per_dr, blocked], axis=1)
    return jnp.concatenate([ext[:, :-1], ext[:, 1:]], axis=-1)


def _nbr_row_masks(rows):
    n_groups = rows // NBR_QR
    assert n_groups >= 3
    masks = []
    for gg in (0, 1, n_groups - 1):
        r = gg * NBR_QR + np.arange(NBR_QR)
        ws = int(np.clip(gg * NBR_QR - WIN_ROWS // 2, 0, rows - NBR_KR))
        rs = np.clip(r - WIN_ROWS // 2, 0, rows - WIN_ROWS)
        rk = ws + np.arange(NBR_KR)
        ok = (rk[None, :] >= rs[:, None]) & (rk[None, :] < rs[:, None] + WIN_ROWS)
        assert ok.sum(axis=1).min() == WIN_ROWS
        m = np.where(ok, 0.0, NEG).astype(np.float32)
        masks.append(np.repeat(np.repeat(m, GRID_W, axis=0), GRID_W, axis=1))
    return jnp.asarray(np.stack(masks))


def _nbr_kernel(q_ref, *refs, rows):
    k_refs = refs[:NBR_HEADS]
    v_refs = refs[NBR_HEADS:2 * NBR_HEADS]
    pair_ref, mask_ref, o_ref = refs[2 * NBR_HEADS:]
    rb = pl.program_id(2)
    n_groups = rows // NBR_QR
    tiles = []
    for g in range(NBR_GROUPS):
        gg = rb * NBR_GROUPS + g
        ws = jnp.clip(gg * NBR_QR - WIN_ROWS // 2, 0, rows - NBR_KR)
        pattern = jnp.where(gg == 0, 0, jnp.where(gg == n_groups - 1, 2, 1))
        d0 = ws - gg * NBR_QR + NBR_MAX_OFF
        start = pl.multiple_of(ws * GRID_W, GRID_W)
        qrows = slice(g * NBR_GQ, (g + 1) * NBR_GQ)
        for hh in range(NBR_HEADS):
            tiles.append((hh, qrows, start, d0, pattern))

    def scores(tile):
        hh, qrows, start, _, _ = tile
        k = k_refs[hh][0, 0, pl.ds(start, NBR_GK), :]
        return lax.dot_general(q_ref[0, hh, qrows, :], k, (((1,), (1,)), ((), ())), preferred_element_type=F32)

    s_next = scores(tiles[0])
    for i, (hh, qrows, start, d0, pattern) in enumerate(tiles):
        s = s_next
        if i + 1 < len(tiles):
            s_next = scores(tiles[i + 1])
        bias = jnp.concatenate(
            [jnp.concatenate([pair_ref[hh, d0 + 2 * c - rq] for c in range(NBR_KR // 2)], axis=1)
             for rq in range(NBR_QR)], axis=0)
        s = s + bias + mask_ref[pattern]
        m = jnp.max(s, axis=-1, keepdims=True)
        p = jnp.exp2(s - m)
        l = jnp.sum(p, axis=-1, keepdims=True)
        v = v_refs[hh][0, 0, pl.ds(start, NBR_GK), :]
        o = jnp.dot(p.astype(BF16), v, preferred_element_type=F32) / l
        o_ref[0, qrows, hh * HEAD_DIM:(hh + 1) * HEAD_DIM] = o.astype(BF16)


def _nbr(qkv, pair, rowmask, batch, seq):
    rows = seq // GRID_W
    n_off = pair.shape[1]
    hw = NBR_HEADS * HEAD_DIM

    def head_block(cb, hh):
        return pl.BlockSpec((1, 1, seq, HEAD_DIM), lambda b, h, i: (b, cb + NBR_HEADS * h + hh, 0, 0))

    return pl.pallas_call(
        functools.partial(_nbr_kernel, rows=rows),
        grid=(batch, N_HEADS_B // NBR_HEADS, rows // NBR_ROWS),
        in_specs=[pl.BlockSpec((1, NBR_HEADS, NBR_TQ, HEAD_DIM), lambda b, h, i: (b, CB_QB // NBR_HEADS + h, i, 0))]
        + [head_block(CB_KB, hh) for hh in range(NBR_HEADS)]
        + [head_block(CB_VB, hh) for hh in range(NBR_HEADS)]
        + [pl.BlockSpec((NBR_HEADS, n_off, GRID_W, 2 * GRID_W), lambda b, h, i: (h, 0, 0, 0)),
           pl.BlockSpec((3, NBR_GQ, NBR_GK), lambda b, h, i: (0, 0, 0))],
        out_specs=pl.BlockSpec((1, NBR_TQ, hw), lambda b, h, i: (b, i, h)),
        out_shape=jax.ShapeDtypeStruct((batch, seq, WIDTH_B), BF16),
        compiler_params=_params(3),
        name="nbr",
    )(qkv, *([qkv] * (2 * NBR_HEADS)), pair, rowmask)


def _merge_kernel(h_ref, ya_ref, yb_ref, gate_ref, wa_ref, wb_ref, wo_ref, o_ref):
    for r in range(0, MERGE_TM, MERGE_RC):
        rows = slice(r, r + MERGE_RC)
        ta = jnp.dot(ya_ref[rows, :], wa_ref[...], preferred_element_type=F32)
        tb = jnp.dot(yb_ref[rows, :], wb_ref[...], preferred_element_type=F32)
        merged = gate_ref[rows, :D_MODEL] * ta + gate_ref[rows, D_MODEL:] * tb
        o_ref[rows, :] = h_ref[rows, :] + jnp.dot(merged.astype(BF16), wo_ref[...], preferred_element_type=F32)


def _merge(h, ya, yb, gates, wa, wb, wo):
    n = h.shape[0]
    tm = MERGE_TM
    row = lambda i: (i, 0)
    fixed = lambda i: (0, 0)
    return pl.pallas_call(
        _merge_kernel,
        grid=(n // tm,),
        in_specs=[
            pl.BlockSpec((tm, D_MODEL), row),
            pl.BlockSpec((tm, WIDTH_QA), row),
            pl.BlockSpec((tm, WIDTH_B), row),
            pl.BlockSpec((tm, GATE_WIDTH), row),
            pl.BlockSpec((WIDTH_QA, D_MODEL), fixed, pipeline_mode=pl.Buffered(1)),
            pl.BlockSpec((WIDTH_B, D_MODEL), fixed, pipeline_mode=pl.Buffered(1)),
            pl.BlockSpec((D_MODEL, D_MODEL), fixed, pipeline_mode=pl.Buffered(1)),
        ],
        out_specs=pl.BlockSpec((tm, D_MODEL), row),
        out_shape=jax.ShapeDtypeStruct((n, D_MODEL), F32),
        compiler_params=_params(1),
        name="merge_out",
    )(h, ya, yb, gates, wa, wb, wo)


def _rope_tables(seq):
    t = np.arange(seq)
    row = (t // GRID_W).astype(np.float32)
    col = (t % GRID_W).astype(np.float32)
    n_pairs_axis = HEAD_DIM // 4
    inv = (ROPE_THETA ** (-np.arange(n_pairs_axis, dtype=np.float32) / n_pairs_axis)).astype(np.float32)
    ang = np.concatenate([row[:, None] * inv[None], col[:, None] * inv[None]], axis=-1)
    cos = np.cos(ang).astype(np.float32)
    sin = np.sin(ang).astype(np.float32)
    cos2 = np.repeat(cos, 2, axis=-1)
    sin2 = np.stack([-sin, sin], axis=-1).reshape(seq, HEAD_DIM)
    return jnp.asarray(cos2), jnp.asarray(sin2)


def _ffn_weights(w_in, w_out):
    return _cast_cols(w_in, 2, FF_TILE, D_FF_PAD // FF_TILE), _cast_rows(w_out, D_FF_PAD)


def kernel(x_prompt, x_sample, g_ffn1, w_ffn1_in, w_ffn1_out, g_mix, w_in, b_gate, g_q_a, g_k_a, rpb_b,
           w_branch_a, w_branch_b, w_out, g_ffn2, w_ffn2_in, w_ffn2_out, g_final):
    assert g_ffn1.shape[0] == 1, "single-layer stack"
    w1 = _ffn_weights(w_ffn1_in[0], w_ffn1_out[0])
    w2 = _ffn_weights(w_ffn2_in[0], w_ffn2_out[0])
    w_proj = _cast_cols(w_in[0], 1, PROJ_TN, IN_WIDTH // PROJ_TN)
    wa = w_branch_a[0].astype(BF16)
    wb = w_branch_b[0].astype(BF16)
    wo = w_out[0].astype(BF16)
    g1 = g_ffn1[0][None, :]
    gm = g_mix[0][None, :]
    g2 = g_ffn2[0][None, :]
    gf = g_final[None, :]
    gq = g_q_a[0][None, :]
    gk = g_k_a[0][None, :]
    bg = b_gate[0][None, :]
    pair = _nbr_pair_table(rpb_b[0])

    def trunk(x):
        batch, seq, _ = x.shape
        n = batch * seq
        cos2, sin2 = _rope_tables(seq)
        rowmask = _nbr_row_masks(seq // GRID_W)
        h, u = _ffn(x.reshape(n, D_MODEL), g1, *w1, gm, final=False)
        qkv = _proj(u, w_proj, gq, gk, cos2, sin2, batch, seq)
        gates = _gates(u, w_proj, bg, seq)
        ya = _gqa(qkv, batch, seq).reshape(n, WIDTH_QA)
        yb = _nbr(qkv, pair, rowmask, batch, seq).reshape(n, WIDTH_B)
        h2 = _merge(h, ya, yb, gates, wa, wb, wo)
        out = _ffn(h2, g2, *w2, gf, final=True)
        return out.reshape(batch, seq, D_MODEL)

    return (trunk(x_prompt), trunk(x_sample))
```

```python
import functools

import numpy as np
import jax
import jax.numpy as jnp
from jax import lax
from jax.experimental import pallas as pl
from jax.experimental.pallas import tpu as pltpu

D_MODEL = 2048
HEAD_DIM = 128
N_HEADS_A = 8
N_KV_HEADS_A = 2
GROUP_A = N_HEADS_A // N_KV_HEADS_A
N_HEADS_B = 8
GRID_W = 64
WIN_ROWS = 8
WIN_COLS = 16
D_FF = 5504
ROPE_THETA = 10000.0
EPS = 1e-6
NEG = -1e30

WIDTH_QA = N_HEADS_A * HEAD_DIM
WIDTH_KVA = N_KV_HEADS_A * HEAD_DIM
WIDTH_B = N_HEADS_B * HEAD_DIM
QKV_WIDTH = WIDTH_QA + 2 * WIDTH_KVA + 3 * WIDTH_B
GATE_WIDTH = 2 * D_MODEL
IN_WIDTH = QKV_WIDTH + GATE_WIDTH
N_QKV_HEADS = QKV_WIDTH // HEAD_DIM

CB_QA = 0
CB_KA = WIDTH_QA // HEAD_DIM
CB_VA = CB_KA + N_KV_HEADS_A
CB_QB = CB_VA + N_KV_HEADS_A
CB_KB = CB_QB + N_HEADS_B
CB_VB = CB_KB + N_HEADS_B

V7X_VMEM_BYTES = 64 * 1024 * 1024
VMEM_LIMIT = V7X_VMEM_BYTES - 2 * 1024 * 1024

FF_TILE = 512
FF_CHUNKS = -(-D_FF // FF_TILE)
FF_PER_STEP = 1
FF_STEPS = -(-FF_CHUNKS // FF_PER_STEP)
FF_LAST = FF_CHUNKS - FF_PER_STEP * (FF_STEPS - 1)
D_FF_PAD = FF_STEPS * FF_PER_STEP * FF_TILE
FFN_TM = 1024
CAST_ROWS = 128
CAST_COLS = 512
PROJ_TM = 2048
PROJ_TN = 512
PROJ_RC = 512
GATE_BLOCKS = 2
MERGE_TM = 512
MERGE_RC = 256
GQA_TQ = 512
GQA_STREAMS = 4
GQA_TK = 512
NBR_ROWS = 16
NBR_QR = 4
NBR_GROUPS = NBR_ROWS // NBR_QR
NBR_KR = 12
NBR_HEADS = 2
NBR_TQ = NBR_ROWS * GRID_W
NBR_GQ = NBR_QR * GRID_W
NBR_GK = NBR_KR * GRID_W
NBR_MAX_OFF = NBR_KR - 1

LOG2E = float(np.log2(np.e))
QK_SCALE = HEAD_DIM ** -0.5 * LOG2E

F32 = jnp.float32
BF16 = jnp.bfloat16


def _params(n_axes):
    return pltpu.CompilerParams(dimension_semantics=("arbitrary",) * n_axes,
                                vmem_limit_bytes=VMEM_LIMIT)


def _rms(xf, g):
    ms = jnp.mean(xf * xf, axis=-1, keepdims=True)
    return xf * lax.rsqrt(ms + EPS) * g


def _cast_cols_kernel(w_ref, o_ref, *, valid):
    n_chunks, rows, tn = o_ref.shape
    for c in range(n_chunks):
        lo = c * tn
        width = max(0, min(tn, valid - lo))
        if width > 0:
            o_ref[c, :, :width] = w_ref[:, lo:lo + width].astype(BF16)
        if width < tn:
            o_ref[c, :, width:] = jnp.zeros((rows, tn - width), BF16)


def _cast_cols(w, n_parts, tn, n_chunks):
    r, total = w.shape
    valid = total // n_parts
    assert n_chunks * tn >= valid
    return pl.pallas_call(
        functools.partial(_cast_cols_kernel, valid=valid),
        grid=(n_parts, r // CAST_ROWS),
        in_specs=[pl.BlockSpec((CAST_ROWS, valid), lambda p, i: (i, p))],
        out_specs=pl.BlockSpec((None, n_chunks, CAST_ROWS, tn), lambda p, i: (p, 0, i, 0)),
        out_shape=jax.ShapeDtypeStruct((n_parts, n_chunks, r, tn), BF16),
        compiler_params=_params(2),
        name="cast_cols",
    )(w)


def _cast_proj_kernel(w_ref, perm_ref, o_ref):
    n_chunks, _, tn = o_ref.shape
    for c in range(n_chunks):
        cols = w_ref[:, c * tn:(c + 1) * tn].astype(BF16)
        if c < perm_ref.shape[0]:
            cols = jnp.dot(cols, perm_ref[c], preferred_element_type=F32).astype(BF16)
        o_ref[c] = cols


def _rope_column_perms():
    head = np.concatenate([np.arange(0, HEAD_DIM, 2), np.arange(1, HEAD_DIM, 2)])
    heads_per_tile = PROJ_TN // HEAD_DIM
    n_rope_heads = N_HEADS_A + N_KV_HEADS_A
    n_tiles = -(-n_rope_heads // heads_per_tile)
    mats = np.zeros((n_tiles, PROJ_TN, PROJ_TN), np.float32)
    for t in range(n_tiles):
        for hh in range(heads_per_tile):
            src = head if t * heads_per_tile + hh < n_rope_heads else np.arange(HEAD_DIM)
            mats[t, hh * HEAD_DIM + src, hh * HEAD_DIM + np.arange(HEAD_DIM)] = 1.0
    return head, jnp.asarray(mats, BF16)


def _cast_proj(w, perms):
    r, total = w.shape
    n_chunks = total // PROJ_TN
    return pl.pallas_call(
        _cast_proj_kernel,
        grid=(r // CAST_ROWS,),
        in_specs=[pl.BlockSpec((CAST_ROWS, total), lambda i: (i, 0)),
                  pl.BlockSpec(perms.shape, lambda i: (0, 0, 0))],
        out_specs=pl.BlockSpec((None, n_chunks, CAST_ROWS, PROJ_TN), lambda i: (0, 0, i, 0)),
        out_shape=jax.ShapeDtypeStruct((1, n_chunks, r, PROJ_TN), BF16),
        compiler_params=_params(1),
        name="cast_proj",
    )(w, perms)


def _cast_rows_kernel(w_ref, o_ref):
    r = w_ref.shape[0]
    o_ref[:r, :] = w_ref[...].astype(BF16)
    o_ref[r:, :] = jnp.zeros((o_ref.shape[0] - r, o_ref.shape[1]), BF16)


def _cast_rows(w, rows_out):
    r, c = w.shape
    return pl.pallas_call(
        _cast_rows_kernel,
        grid=(c // CAST_COLS,),
        in_specs=[pl.BlockSpec((r, CAST_COLS), lambda i: (0, i))],
        out_specs=pl.BlockSpec((rows_out, CAST_COLS), lambda i: (0, i)),
        out_shape=jax.ShapeDtypeStruct((rows_out, c), BF16),
        compiler_params=_params(1),
        name="cast_rows",
    )(w)


def _ffn_kernel(x_ref, g_ref, wgu_ref, wo_ref, g2_ref, *refs, final):
    if final:
        acc_ref, xn_ref = refs
    else:
        acc_ref, u_ref, xn_ref = refs
    j = pl.program_id(1)
    last = pl.num_programs(1) - 1

    @pl.when(j == 0)
    def _():
        xn_ref[...] = _rms(x_ref[...], g_ref[...]).astype(BF16)
        acc_ref[...] = jnp.zeros_like(acc_ref)

    def chunk(c):
        xn = xn_ref[...]
        gate = jnp.dot(xn, wgu_ref[0, c], preferred_element_type=F32)
        up = jnp.dot(xn, wgu_ref[1, c], preferred_element_type=F32)
        act = ((gate * jax.nn.sigmoid(gate)) * up).astype(BF16)
        acc_ref[...] += jnp.dot(act, wo_ref[c * FF_TILE:(c + 1) * FF_TILE, :], preferred_element_type=F32)

    @pl.when(j < last)
    def _():
        for c in range(FF_PER_STEP):
            chunk(c)

    @pl.when(j == last)
    def _():
        for c in range(FF_LAST):
            chunk(c)
        h = x_ref[...] + 0.5 * acc_ref[...]
        if final:
            acc_ref[...] = _rms(h, g2_ref[...])
        else:
            acc_ref[...] = h
            u_ref[...] = _rms(h, g2_ref[...]).astype(BF16)


def _ffn(x, g, w_gu, wo, g2, *, final):
    n = x.shape[0]
    tm = FFN_TM
    row = lambda i, j: (i, 0)
    in_specs = [
        pl.BlockSpec((tm, D_MODEL), row),
        pl.BlockSpec((1, D_MODEL), lambda i, j: (0, 0)),
        pl.BlockSpec((2, FF_PER_STEP, D_MODEL, FF_TILE), lambda i, j: (0, j, 0, 0)),
        pl.BlockSpec((FF_PER_STEP * FF_TILE, D_MODEL), lambda i, j: (j, 0)),
        pl.BlockSpec((1, D_MODEL), lambda i, j: (0, 0)),
    ]
    if final:
        out_shape = jax.ShapeDtypeStruct((n, D_MODEL), F32)
        out_specs = pl.BlockSpec((tm, D_MODEL), row)
    else:
        out_shape = (jax.ShapeDtypeStruct((n, D_MODEL), F32), jax.ShapeDtypeStruct((n, D_MODEL), BF16))
        out_specs = (pl.BlockSpec((tm, D_MODEL), row), pl.BlockSpec((tm, D_MODEL), row))
    return pl.pallas_call(
        functools.partial(_ffn_kernel, final=final),
        grid=(n // tm, FF_STEPS),
        in_specs=in_specs,
        out_specs=out_specs,
        out_shape=out_shape,
        scratch_shapes=[pltpu.VMEM((tm, D_MODEL), BF16)],
        compiler_params=_params(2),
        name="ffn_final" if final else "ffn_first",
    )(x, g, w_gu, wo, g2)


def _rope(y, cos2, sin2):
    return y * cos2 + pltpu.roll(y, HEAD_DIM // 2, 1) * sin2


def _row_chunks(n_rows):
    return [slice(r, r + PROJ_RC) for r in range(0, n_rows, PROJ_RC)]


def _proj_kernel(u_ref, w_ref, gq_ref, gk_ref, cos_ref, sin_ref, o_ref):
    j = pl.program_id(1)
    heads_per_tile = PROJ_TN // HEAD_DIM
    qa_tiles = WIDTH_QA // PROJ_TN
    kva_tile = qa_tiles
    qb_lo = (CB_QB * HEAD_DIM) // PROJ_TN
    qb_hi = (CB_KB * HEAD_DIM) // PROJ_TN

    def normed_rope(yh, g_ref, rows):
        return _rope(_rms(yh, g_ref[...]), cos_ref[rows, :], sin_ref[rows, :])

    def for_heads(epilogue):
        for rows in _row_chunks(u_ref.shape[0]):
            y = jnp.dot(u_ref[rows, :], w_ref[...], preferred_element_type=F32)
            for hh in range(heads_per_tile):
                o_ref[0, hh, rows, :] = epilogue(hh, rows, y[:, hh * HEAD_DIM:(hh + 1) * HEAD_DIM]).astype(BF16)

    @pl.when(j < qa_tiles)
    def _():
        for_heads(lambda hh, rows, yh: normed_rope(yh, gq_ref, rows) * QK_SCALE)

    @pl.when(j == kva_tile)
    def _():
        for_heads(lambda hh, rows, yh: normed_rope(yh, gk_ref, rows) if hh < N_KV_HEADS_A else yh)

    @pl.when((j >= qb_lo) & (j < qb_hi))
    def _():
        for_heads(lambda hh, rows, yh: yh * QK_SCALE)

    @pl.when(j >= qb_hi)
    def _():
        for_heads(lambda hh, rows, yh: yh)


def _proj(u, w_blocks, gq, gk, cos2, sin2, batch, seq):
    n = u.shape[0]
    tm = min(PROJ_TM, seq)
    pos_tiles = seq // tm
    heads_per_tile = PROJ_TN // HEAD_DIM
    return pl.pallas_call(
        _proj_kernel,
        grid=(n // tm, QKV_WIDTH // PROJ_TN),
        in_specs=[
            pl.BlockSpec((tm, D_MODEL), lambda i, j: (i, 0)),
            pl.BlockSpec((None, None, D_MODEL, PROJ_TN), lambda i, j: (0, j, 0, 0)),
            pl.BlockSpec((1, HEAD_DIM), lambda i, j: (0, 0)),
            pl.BlockSpec((1, HEAD_DIM), lambda i, j: (0, 0)),
            pl.BlockSpec((tm, HEAD_DIM), lambda i, j: (i % pos_tiles, 0)),
            pl.BlockSpec((tm, HEAD_DIM), lambda i, j: (i % pos_tiles, 0)),
        ],
        out_specs=pl.BlockSpec((1, heads_per_tile, tm, HEAD_DIM),
                               lambda i, j: (i // pos_tiles, j, i % pos_tiles, 0)),
        out_shape=jax.ShapeDtypeStruct((batch, N_QKV_HEADS, seq, HEAD_DIM), BF16),
        compiler_params=_params(2),
        name="proj_qkv",
    )(u, w_blocks, gq, gk, cos2, sin2)


def _gate_kernel(u_ref, *refs):
    w_refs, (b_ref, o_ref) = refs[:GATE_BLOCKS], refs[GATE_BLOCKS:]
    for rows in _row_chunks(u_ref.shape[0]):
        for c, w_ref in enumerate(w_refs):
            cols = slice(c * PROJ_TN, (c + 1) * PROJ_TN)
            y = jnp.dot(u_ref[rows, :], w_ref[...], preferred_element_type=F32)
            o_ref[rows, cols] = 0.5 * jnp.tanh(0.5 * (y + b_ref[:, cols])) + 0.5


def _gates(u, w_blocks, b_gate, seq):
    n = u.shape[0]
    tm = min(PROJ_TM, seq)
    first = QKV_WIDTH // PROJ_TN
    width = GATE_BLOCKS * PROJ_TN

    def chunk(c):
        return pl.BlockSpec((None, None, D_MODEL, PROJ_TN), lambda i, j: (0, first + GATE_BLOCKS * j + c, 0, 0))

    return pl.pallas_call(
        _gate_kernel,
        grid=(n // tm, GATE_WIDTH // width),
        in_specs=[pl.BlockSpec((tm, D_MODEL), lambda i, j: (i, 0))]
        + [chunk(c) for c in range(GATE_BLOCKS)]
        + [pl.BlockSpec((1, width), lambda i, j: (0, j))],
        out_specs=pl.BlockSpec((tm, width), lambda i, j: (i, j)),
        out_shape=jax.ShapeDtypeStruct((n, GATE_WIDTH), F32),
        compiler_params=_params(2),
        name="proj_gates",
    )(u, *([w_blocks] * GATE_BLOCKS), b_gate)


def _gqa_kernel(q_ref, k_ref, v_ref, o_ref, *, seq):
    ts = GQA_TQ // GQA_STREAMS
    rows = GROUP_A * ts
    qs, ms, ls, accs = [], [], [], []
    for t in range(GQA_STREAMS):
        tok = slice(t * ts, (t + 1) * ts)
        q = jnp.concatenate([q_ref[0, g, tok, :] for g in range(GROUP_A)], axis=0)
        qs.append(q.T)
        ms.append(jnp.full((1, rows), -jnp.inf, F32))
        ls.append(jnp.zeros((1, rows), F32))
        accs.append(jnp.zeros((HEAD_DIM, rows), F32))
    n_chunks = seq // GQA_TK

    def scores(c, t):
        k = k_ref[0, 0, c * GQA_TK:(c + 1) * GQA_TK, :]
        return jnp.dot(k, qs[t], preferred_element_type=F32)

    work = [(c, t) for c in range(n_chunks) for t in range(GQA_STREAMS)]
    st_next = scores(*work[0])
    for i, (c, t) in enumerate(work):
        st = st_next
        if i + 1 < len(work):
            st_next = scores(*work[i + 1])
        v = v_ref[0, 0, c * GQA_TK:(c + 1) * GQA_TK, :]
        m_new = jnp.maximum(ms[t], jnp.max(st, axis=0, keepdims=True))
        alpha = jnp.exp2(ms[t] - m_new)
        p = jnp.exp2(st - m_new)
        ls[t] = alpha * ls[t] + jnp.sum(p, axis=0, keepdims=True)
        pv = lax.dot_general(v, p.astype(BF16), (((0,), (0,)), ((), ())), preferred_element_type=F32)
        accs[t] = alpha * accs[t] + pv
        ms[t] = m_new
    for t in range(GQA_STREAMS):
        o = (accs[t] / ls[t]).T
        for g in range(GROUP_A):
            o_ref[0, t * ts:(t + 1) * ts, g * HEAD_DIM:(g + 1) * HEAD_DIM] = o[g * ts:(g + 1) * ts].astype(BF16)


def _gqa(qkv, batch, seq):
    gw = GROUP_A * HEAD_DIM
    return pl.pallas_call(
        functools.partial(_gqa_kernel, seq=seq),
        grid=(batch, N_KV_HEADS_A, seq // GQA_TQ),
        in_specs=[
            pl.BlockSpec((1, GROUP_A, GQA_TQ, HEAD_DIM), lambda b, h, i: (b, h, i, 0)),
            pl.BlockSpec((1, 1, seq, HEAD_DIM), lambda b, h, i: (b, CB_KA + h, 0, 0)),
            pl.BlockSpec((1, 1, seq, HEAD_DIM), lambda b, h, i: (b, CB_VA + h, 0, 0)),
        ],
        out_specs=pl.BlockSpec((1, GQA_TQ, gw), lambda b, h, i: (b, i, h)),
        out_shape=jax.ShapeDtypeStruct((batch, seq, WIDTH_QA), BF16),
        compiler_params=_params(3),
        name="gqa",
    )(qkv, qkv, qkv)


def _nbr_pair_table(rpb):
    qc = np.arange(GRID_W)[:, None]
    kc = np.arange(GRID_W)[None, :]
    cs = np.clip(qc - WIN_COLS // 2, 0, GRID_W - WIN_COLS)
    col_ok = (kc >= cs) & (kc < cs + WIN_COLS)
    dc = np.clip(kc - qc + WIN_COLS - 1, 0, 2 * WIN_COLS - 2)
    onehot = (dc.reshape(1, -1) == np.arange(2 * WIN_COLS - 1)[:, None]).astype(np.float32)
    n_dr = 2 * WIN_ROWS - 1
    picked = jnp.dot(rpb.astype(F32).reshape(N_HEADS_B * n_dr, -1), jnp.asarray(onehot),
                     precision=lax.Precision.HIGHEST).reshape(N_HEADS_B, n_dr, GRID_W, GRID_W)
    per_dr = jnp.where(col_ok[None, None], picked * LOG2E, NEG)
    pad = NBR_MAX_OFF - (WIN_ROWS - 1)
    blocked = jnp.full((N_HEADS_B, pad, GRID_W, GRID_W), NEG, F32)
    ext = jnp.concatenate([blocked, per_dr, blocked], axis=1)
    return jnp.concatenate([ext[:, :-1], ext[:, 1:]], axis=-1)


def _nbr_row_masks(rows):
    n_groups = rows // NBR_QR
    assert n_groups >= 3
    masks = []
    for gg in (0, 1, n_groups - 1):
        r = gg * NBR_QR + np.arange(NBR_QR)
        ws = int(np.clip(gg * NBR_QR - WIN_ROWS // 2, 0, rows - NBR_KR))
        rs = np.clip(r - WIN_ROWS // 2, 0, rows - WIN_ROWS)
        rk = ws + np.arange(NBR_KR)
        ok = (rk[None, :] >= rs[:, None]) & (rk[None, :] < rs[:, None] + WIN_ROWS)
        assert ok.sum(axis=1).min() == WIN_ROWS
        m = np.where(ok, 0.0, NEG).astype(np.float32)
        masks.append(np.repeat(np.repeat(m, GRID_W, axis=0), GRID_W, axis=1))
    return jnp.asarray(np.stack(masks))


def _nbr_kernel(q_ref, *refs, rows):
    k_refs = refs[:NBR_HEADS]
    v_refs = refs[NBR_HEADS:2 * NBR_HEADS]
    pair_ref, mask_ref, o_ref = refs[2 * NBR_HEADS:]
    rb = pl.program_id(2)
    n_groups = rows // NBR_QR
    tiles = []
    for g in range(NBR_GROUPS):
        gg = rb * NBR_GROUPS + g
        ws = jnp.clip(gg * NBR_QR - WIN_ROWS // 2, 0, rows - NBR_KR)
        pattern = jnp.where(gg == 0, 0, jnp.where(gg == n_groups - 1, 2, 1))
        d0 = ws - gg * NBR_QR + NBR_MAX_OFF
        start = pl.multiple_of(ws * GRID_W, GRID_W)
        qrows = slice(g * NBR_GQ, (g + 1) * NBR_GQ)
        for hh in range(NBR_HEADS):
            tiles.append((hh, qrows, start, d0, pattern))

    def scores(tile):
        hh, qrows, start, _, _ = tile
        k = k_refs[hh][0, 0, pl.ds(start, NBR_GK), :]
        return lax.dot_general(q_ref[0, hh, qrows, :], k, (((1,), (1,)), ((), ())), preferred_element_type=F32)

    s_next = scores(tiles[0])
    for i, (hh, qrows, start, d0, pattern) in enumerate(tiles):
        s = s_next
        if i + 1 < len(tiles):
            s_next = scores(tiles[i + 1])
        bias = jnp.concatenate(
            [jnp.concatenate([pair_ref[hh, d0 + 2 * c - rq] for c in range(NBR_KR // 2)], axis=1)
             for rq in range(NBR_QR)], axis=0)
        s = s + bias + mask_ref[pattern]
        m = jnp.max(s, axis=-1, keepdims=True)
        p = jnp.exp2(s - m)
        l = jnp.sum(p, axis=-1, keepdims=True)
        v = v_refs[hh][0, 0, pl.ds(start, NBR_GK), :]
        o = jnp.dot(p.astype(BF16), v, preferred_element_type=F32) / l
        o_ref[0, qrows, hh * HEAD_DIM:(hh + 1) * HEAD_DIM] = o.astype(BF16)


def _nbr(qkv, pair, rowmask, batch, seq):
    rows = seq // GRID_W
    n_off = pair.shape[1]
    hw = NBR_HEADS * HEAD_DIM

    def head_block(cb, hh):
        return pl.BlockSpec((1, 1, seq, HEAD_DIM), lambda b, h, i: (b, cb + NBR_HEADS * h + hh, 0, 0))

    return pl.pallas_call(
        functools.partial(_nbr_kernel, rows=rows),
        grid=(batch, N_HEADS_B // NBR_HEADS, rows // NBR_ROWS),
        in_specs=[pl.BlockSpec((1, NBR_HEADS, NBR_TQ, HEAD_DIM), lambda b, h, i: (b, CB_QB // NBR_HEADS + h, i, 0))]
        + [head_block(CB_KB, hh) for hh in range(NBR_HEADS)]
        + [head_block(CB_VB, hh) for hh in range(NBR_HEADS)]
        + [pl.BlockSpec((NBR_HEADS, n_off, GRID_W, 2 * GRID_W), lambda b, h, i: (h, 0, 0, 0)),
           pl.BlockSpec((3, NBR_GQ, NBR_GK), lambda b, h, i: (0, 0, 0))],
        out_specs=pl.BlockSpec((1, NBR_TQ, hw), lambda b, h, i: (b, i, h)),
        out_shape=jax.ShapeDtypeStruct((batch, seq, WIDTH_B), BF16),
        compiler_params=_params(3),
        name="nbr",
    )(qkv, *([qkv] * (2 * NBR_HEADS)), pair, rowmask)


def _merge_kernel(h_ref, ya_ref, yb_ref, gate_ref, wa_ref, wb_ref, wo_ref, o_ref):
    for r in range(0, MERGE_TM, MERGE_RC):
        rows = slice(r, r + MERGE_RC)
        ta = jnp.dot(ya_ref[rows, :], wa_ref[...], preferred_element_type=F32)
        tb = jnp.dot(yb_ref[rows, :], wb_ref[...], preferred_element_type=F32)
        merged = gate_ref[rows, :D_MODEL] * ta + gate_ref[rows, D_MODEL:] * tb
        o_ref[rows, :] = h_ref[rows, :] + jnp.dot(merged.astype(BF16), wo_ref[...], preferred_element_type=F32)


def _merge(h, ya, yb, gates, wa, wb, wo):
    n = h.shape[0]
    tm = MERGE_TM
    row = lambda i: (i, 0)
    fixed = lambda i: (0, 0)
    return pl.pallas_call(
        _merge_kernel,
        grid=(n // tm,),
        in_specs=[
            pl.BlockSpec((tm, D_MODEL), row),
            pl.BlockSpec((tm, WIDTH_QA), row),
            pl.BlockSpec((tm, WIDTH_B), row),
            pl.BlockSpec((tm, GATE_WIDTH), row),
            pl.BlockSpec((WIDTH_QA, D_MODEL), fixed, pipeline_mode=pl.Buffered(1)),
            pl.BlockSpec((WIDTH_B, D_MODEL), fixed, pipeline_mode=pl.Buffered(1)),
            pl.BlockSpec((D_MODEL, D_MODEL), fixed, pipeline_mode=pl.Buffered(1)),
        ],
        out_specs=pl.BlockSpec((tm, D_MODEL), row),
        out_shape=jax.ShapeDtypeStruct((n, D_MODEL), F32),
        compiler_params=_params(1),
        name="merge_out",
    )(h, ya, yb, gates, wa, wb, wo)


def _rope_tables(seq):
    t = np.arange(seq)
    row = (t // GRID_W).astype(np.float32)
    col = (t % GRID_W).astype(np.float32)
    n_pairs_axis = HEAD_DIM // 4
    inv = (ROPE_THETA ** (-np.arange(n_pairs_axis, dtype=np.float32) / n_pairs_axis)).astype(np.float32)
    ang = np.concatenate([row[:, None] * inv[None], col[:, None] * inv[None]], axis=-1)
    cos = np.cos(ang).astype(np.float32)
    sin = np.sin(ang).astype(np.float32)
    cos2 = np.concatenate([cos, cos], axis=-1)
    sin2 = np.concatenate([-sin, sin], axis=-1)
    return jnp.asarray(cos2), jnp.asarray(sin2)


def _ffn_weights(w_in, w_out):
    return _cast_cols(w_in, 2, FF_TILE, D_FF_PAD // FF_TILE), _cast_rows(w_out, D_FF_PAD)


def kernel(x_prompt, x_sample, g_ffn1, w_ffn1_in, w_ffn1_out, g_mix, w_in, b_gate, g_q_a, g_k_a, rpb_b,
           w_branch_a, w_branch_b, w_out, g_ffn2, w_ffn2_in, w_ffn2_out, g_final):
    assert g_ffn1.shape[0] == 1, "single-layer stack"
    w1 = _ffn_weights(w_ffn1_in[0], w_ffn1_out[0])
    w2 = _ffn_weights(w_ffn2_in[0], w_ffn2_out[0])
    head_order, perms = _rope_column_perms()
    w_proj = _cast_proj(w_in[0], perms)
    wa = w_branch_a[0].astype(BF16)
    wb = w_branch_b[0].astype(BF16)
    wo = w_out[0].astype(BF16)
    g1 = g_ffn1[0][None, :]
    gm = g_mix[0][None, :]
    g2 = g_ffn2[0][None, :]
    gf = g_final[None, :]
    gq = g_q_a[0][head_order][None, :]
    gk = g_k_a[0][head_order][None, :]
    bg = b_gate[0][None, :]
    pair = _nbr_pair_table(rpb_b[0])

    def trunk(x):
        batch, seq, _ = x.shape
        n = batch * seq
        cos2, sin2 = _rope_tables(seq)
        rowmask = _nbr_row_masks(seq // GRID_W)
        h, u = _ffn(x.reshape(n, D_MODEL), g1, *w1, gm, final=False)
        qkv = _proj(u, w_proj, gq, gk, cos2, sin2, batch, seq)
        gates = _gates(u, w_proj, bg, seq)
        ya = _gqa(qkv, batch, seq).reshape(n, WIDTH_QA)
        yb = _nbr(qkv, pair, rowmask, batch, seq).reshape(n, WIDTH_B)
        h2 = _merge(h, ya, yb, gates, wa, wb, wo)
        out = _ffn(h2, g2, *w2, gf, final=True)
        return out.reshape(batch, seq, D_MODEL)

    return (trunk(x_prompt), trunk(x_sample))
```

```python
import functools

import numpy as np
import jax
import jax.numpy as jnp
from jax import lax
from jax.experimental import pallas as pl
from jax.experimental.pallas import tpu as pltpu

D_MODEL = 2048
HEAD_DIM = 128
N_HEADS_A = 8
N_KV_HEADS_A = 2
GROUP_A = N_HEADS_A // N_KV_HEADS_A
N_HEADS_B = 8
GRID_W = 64
WIN_ROWS = 8
WIN_COLS = 16
D_FF = 5504
ROPE_THETA = 10000.0
EPS = 1e-6
NEG = -1e30

WIDTH_QA = N_HEADS_A * HEAD_DIM
WIDTH_KVA = N_KV_HEADS_A * HEAD_DIM
WIDTH_B = N_HEADS_B * HEAD_DIM
QKV_WIDTH = WIDTH_QA + 2 * WIDTH_KVA + 3 * WIDTH_B
GATE_WIDTH = 2 * D_MODEL
IN_WIDTH = QKV_WIDTH + GATE_WIDTH
N_QKV_HEADS = QKV_WIDTH // HEAD_DIM

CB_QA = 0
CB_KA = WIDTH_QA // HEAD_DIM
CB_VA = CB_KA + N_KV_HEADS_A
CB_QB = CB_VA + N_KV_HEADS_A
CB_KB = CB_QB + N_HEADS_B
CB_VB = CB_KB + N_HEADS_B

V7X_VMEM_BYTES = 64 * 1024 * 1024
VMEM_LIMIT = V7X_VMEM_BYTES - 2 * 1024 * 1024

FF_TILE = 512
FF_CHUNKS = -(-D_FF // FF_TILE)
FF_PER_STEP = 1
FF_STEPS = -(-FF_CHUNKS // FF_PER_STEP)
FF_LAST = FF_CHUNKS - FF_PER_STEP * (FF_STEPS - 1)
D_FF_PAD = FF_STEPS * FF_PER_STEP * FF_TILE
FFN_TM = 1024
CAST_ROWS = 128
CAST_COLS = 512
PROJ_TM = 2048
PROJ_TN = 512
PROJ_RC = 512
GATE_BLOCKS = 2
MERGE_TM = 512
MERGE_RC = 256
GQA_TQ = 512
GQA_STREAMS = 4
GQA_TK = 1024
NBR_ROWS = 16
NBR_QR = 4
NBR_GROUPS = NBR_ROWS // NBR_QR
NBR_KR = 12
NBR_HEADS = 4
NBR_TQ = NBR_ROWS * GRID_W
NBR_GQ = NBR_QR * GRID_W
NBR_GK = NBR_KR * GRID_W
NBR_MAX_OFF = NBR_KR - 1

LOG2E = float(np.log2(np.e))
QK_SCALE = HEAD_DIM ** -0.5 * LOG2E

F32 = jnp.float32
BF16 = jnp.bfloat16


def _params(n_axes):
    return pltpu.CompilerParams(dimension_semantics=("arbitrary",) * n_axes,
                                vmem_limit_bytes=VMEM_LIMIT)


def _rms(xf, g):
    ms = jnp.mean(xf * xf, axis=-1, keepdims=True)
    return xf * lax.rsqrt(ms + EPS) * g


def _cast_cols_kernel(w_ref, o_ref, *, valid):
    n_chunks, rows, tn = o_ref.shape
    for c in range(n_chunks):
        lo = c * tn
        width = max(0, min(tn, valid - lo))
        if width > 0:
            o_ref[c, :, :width] = w_ref[:, lo:lo + width].astype(BF16)
        if width < tn:
            o_ref[c, :, width:] = jnp.zeros((rows, tn - width), BF16)


def _cast_cols(w, n_parts, tn, n_chunks):
    r, total = w.shape
    valid = total // n_parts
    assert n_chunks * tn >= valid
    return pl.pallas_call(
        functools.partial(_cast_cols_kernel, valid=valid),
        grid=(n_parts, r // CAST_ROWS),
        in_specs=[pl.BlockSpec((CAST_ROWS, valid), lambda p, i: (i, p))],
        out_specs=pl.BlockSpec((None, n_chunks, CAST_ROWS, tn), lambda p, i: (p, 0, i, 0)),
        out_shape=jax.ShapeDtypeStruct((n_parts, n_chunks, r, tn), BF16),
        compiler_params=_params(2),
        name="cast_cols",
    )(w)


def _cast_proj_kernel(w_ref, perm_ref, o_ref):
    n_chunks, _, tn = o_ref.shape
    for c in range(n_chunks):
        cols = w_ref[:, c * tn:(c + 1) * tn].astype(BF16)
        if c < perm_ref.shape[0]:
            cols = jnp.dot(cols, perm_ref[c], preferred_element_type=F32).astype(BF16)
        o_ref[c] = cols


def _rope_column_perms():
    head = np.concatenate([np.arange(0, HEAD_DIM, 2), np.arange(1, HEAD_DIM, 2)])
    heads_per_tile = PROJ_TN // HEAD_DIM
    n_rope_heads = N_HEADS_A + N_KV_HEADS_A
    n_tiles = -(-n_rope_heads // heads_per_tile)
    mats = np.zeros((n_tiles, PROJ_TN, PROJ_TN), np.float32)
    for t in range(n_tiles):
        for hh in range(heads_per_tile):
            src = head if t * heads_per_tile + hh < n_rope_heads else np.arange(HEAD_DIM)
            mats[t, hh * HEAD_DIM + src, hh * HEAD_DIM + np.arange(HEAD_DIM)] = 1.0
    return head, jnp.asarray(mats, BF16)


def _cast_proj(w, perms):
    r, total = w.shape
    n_chunks = total // PROJ_TN
    return pl.pallas_call(
        _cast_proj_kernel,
        grid=(r // CAST_ROWS,),
        in_specs=[pl.BlockSpec((CAST_ROWS, total), lambda i: (i, 0)),
                  pl.BlockSpec(perms.shape, lambda i: (0, 0, 0))],
        out_specs=pl.BlockSpec((None, n_chunks, CAST_ROWS, PROJ_TN), lambda i: (0, 0, i, 0)),
        out_shape=jax.ShapeDtypeStruct((1, n_chunks, r, PROJ_TN), BF16),
        compiler_params=_params(1),
        name="cast_proj",
    )(w, perms)


def _cast_rows_kernel(w_ref, o_ref):
    r = w_ref.shape[0]
    o_ref[:r, :] = w_ref[...].astype(BF16)
    o_ref[r:, :] = jnp.zeros((o_ref.shape[0] - r, o_ref.shape[1]), BF16)


def _cast_rows(w, rows_out):
    r, c = w.shape
    return pl.pallas_call(
        _cast_rows_kernel,
        grid=(c // CAST_COLS,),
        in_specs=[pl.BlockSpec((r, CAST_COLS), lambda i: (0, i))],
        out_specs=pl.BlockSpec((rows_out, CAST_COLS), lambda i: (0, i)),
        out_shape=jax.ShapeDtypeStruct((rows_out, c), BF16),
        compiler_params=_params(1),
        name="cast_rows",
    )(w)


def _ffn_kernel(x_ref, g_ref, wgu_ref, wo_ref, g2_ref, *refs, final):
    if final:
        acc_ref, xn_ref = refs
    else:
        acc_ref, u_ref, xn_ref = refs
    j = pl.program_id(1)
    last = pl.num_programs(1) - 1

    @pl.when(j == 0)
    def _():
        xn_ref[...] = _rms(x_ref[...], g_ref[...]).astype(BF16)
        acc_ref[...] = jnp.zeros_like(acc_ref)

    def chunk(c):
        xn = xn_ref[...]
        gate = jnp.dot(xn, wgu_ref[0, c], preferred_element_type=F32)
        up = jnp.dot(xn, wgu_ref[1, c], preferred_element_type=F32)
        act = ((gate * jax.nn.sigmoid(gate)) * up).astype(BF16)
        acc_ref[...] += jnp.dot(act, wo_ref[c * FF_TILE:(c + 1) * FF_TILE, :], preferred_element_type=F32)

    @pl.when(j < last)
    def _():
        for c in range(FF_PER_STEP):
            chunk(c)

    @pl.when(j == last)
    def _():
        for c in range(FF_LAST):
            chunk(c)
        h = x_ref[...] + 0.5 * acc_ref[...]
        if final:
            acc_ref[...] = _rms(h, g2_ref[...])
        else:
            acc_ref[...] = h
            u_ref[...] = _rms(h, g2_ref[...]).astype(BF16)


def _ffn(x, g, w_gu, wo, g2, *, final):
    n = x.shape[0]
    tm = FFN_TM
    row = lambda i, j: (i, 0)
    in_specs = [
        pl.BlockSpec((tm, D_MODEL), row),
        pl.BlockSpec((1, D_MODEL), lambda i, j: (0, 0)),
        pl.BlockSpec((2, FF_PER_STEP, D_MODEL, FF_TILE), lambda i, j: (0, j, 0, 0)),
        pl.BlockSpec((FF_PER_STEP * FF_TILE, D_MODEL), lambda i, j: (j, 0)),
        pl.BlockSpec((1, D_MODEL), lambda i, j: (0, 0)),
    ]
    if final:
        out_shape = jax.ShapeDtypeStruct((n, D_MODEL), F32)
        out_specs = pl.BlockSpec((tm, D_MODEL), row)
    else:
        out_shape = (jax.ShapeDtypeStruct((n, D_MODEL), F32), jax.ShapeDtypeStruct((n, D_MODEL), BF16))
        out_specs = (pl.BlockSpec((tm, D_MODEL), row), pl.BlockSpec((tm, D_MODEL), row))
    return pl.pallas_call(
        functools.partial(_ffn_kernel, final=final),
        grid=(n // tm, FF_STEPS),
        in_specs=in_specs,
        out_specs=out_specs,
        out_shape=out_shape,
        scratch_shapes=[pltpu.VMEM((tm, D_MODEL), BF16)],
        compiler_params=_params(2),
        name="ffn_final" if final else "ffn_first",
    )(x, g, w_gu, wo, g2)


def _rope(y, cos2, sin2):
    return y * cos2 + pltpu.roll(y, HEAD_DIM // 2, 1) * sin2


def _row_chunks(n_rows):
    return [slice(r, r + PROJ_RC) for r in range(0, n_rows, PROJ_RC)]


def _proj_kernel(u_ref, w_ref, gq_ref, gk_ref, cos_ref, sin_ref, o_ref):
    j = pl.program_id(1)
    heads_per_tile = PROJ_TN // HEAD_DIM
    qa_tiles = WIDTH_QA // PROJ_TN
    kva_tile = qa_tiles
    qb_lo = (CB_QB * HEAD_DIM) // PROJ_TN
    qb_hi = (CB_KB * HEAD_DIM) // PROJ_TN

    def normed_rope(yh, g_ref, rows):
        return _rope(_rms(yh, g_ref[...]), cos_ref[rows, :], sin_ref[rows, :])

    def for_heads(epilogue):
        for rows in _row_chunks(u_ref.shape[0]):
            y = jnp.dot(u_ref[rows, :], w_ref[...], preferred_element_type=F32)
            for hh in range(heads_per_tile):
                o_ref[0, hh, rows, :] = epilogue(hh, rows, y[:, hh * HEAD_DIM:(hh + 1) * HEAD_DIM]).astype(BF16)

    @pl.when(j < qa_tiles)
    def _():
        for_heads(lambda hh, rows, yh: normed_rope(yh, gq_ref, rows) * QK_SCALE)

    @pl.when(j == kva_tile)
    def _():
        for_heads(lambda hh, rows, yh: normed_rope(yh, gk_ref, rows) if hh < N_KV_HEADS_A else yh)

    @pl.when((j >= qb_lo) & (j < qb_hi))
    def _():
        for_heads(lambda hh, rows, yh: yh * QK_SCALE)

    @pl.when(j >= qb_hi)
    def _():
        for_heads(lambda hh, rows, yh: yh)


def _proj(u, w_blocks, gq, gk, cos2, sin2, batch, seq):
    n = u.shape[0]
    tm = min(PROJ_TM, seq)
    pos_tiles = seq // tm
    heads_per_tile = PROJ_TN // HEAD_DIM
    return pl.pallas_call(
        _proj_kernel,
        grid=(n // tm, QKV_WIDTH // PROJ_TN),
        in_specs=[
            pl.BlockSpec((tm, D_MODEL), lambda i, j: (i, 0)),
            pl.BlockSpec((None, None, D_MODEL, PROJ_TN), lambda i, j: (0, j, 0, 0)),
            pl.BlockSpec((1, HEAD_DIM), lambda i, j: (0, 0)),
            pl.BlockSpec((1, HEAD_DIM), lambda i, j: (0, 0)),
            pl.BlockSpec((tm, HEAD_DIM), lambda i, j: (i % pos_tiles, 0)),
            pl.BlockSpec((tm, HEAD_DIM), lambda i, j: (i % pos_tiles, 0)),
        ],
        out_specs=pl.BlockSpec((1, heads_per_tile, tm, HEAD_DIM),
                               lambda i, j: (i // pos_tiles, j, i % pos_tiles, 0)),
        out_shape=jax.ShapeDtypeStruct((batch, N_QKV_HEADS, seq, HEAD_DIM), BF16),
        compiler_params=_params(2),
        name="proj_qkv",
    )(u, w_blocks, gq, gk, cos2, sin2)


def _gate_kernel(u_ref, *refs):
    w_refs, (b_ref, o_ref) = refs[:GATE_BLOCKS], refs[GATE_BLOCKS:]
    for rows in _row_chunks(u_ref.shape[0]):
        for c, w_ref in enumerate(w_refs):
            cols = slice(c * PROJ_TN, (c + 1) * PROJ_TN)
            y = jnp.dot(u_ref[rows, :], w_ref[...], preferred_element_type=F32)
            o_ref[rows, cols] = 0.5 * jnp.tanh(0.5 * (y + b_ref[:, cols])) + 0.5


def _gates(u, w_blocks, b_gate, seq):
    n = u.shape[0]
    tm = min(PROJ_TM, seq)
    first = QKV_WIDTH // PROJ_TN
    width = GATE_BLOCKS * PROJ_TN

    def chunk(c):
        return pl.BlockSpec((None, None, D_MODEL, PROJ_TN), lambda i, j: (0, first + GATE_BLOCKS * j + c, 0, 0))

    return pl.pallas_call(
        _gate_kernel,
        grid=(n // tm, GATE_WIDTH // width),
        in_specs=[pl.BlockSpec((tm, D_MODEL), lambda i, j: (i, 0))]
        + [chunk(c) for c in range(GATE_BLOCKS)]
        + [pl.BlockSpec((1, width), lambda i, j: (0, j))],
        out_specs=pl.BlockSpec((tm, width), lambda i, j: (i, j)),
        out_shape=jax.ShapeDtypeStruct((n, GATE_WIDTH), F32),
        compiler_params=_params(2),
        name="proj_gates",
    )(u, *([w_blocks] * GATE_BLOCKS), b_gate)


def _gqa_kernel(q_ref, k_ref, v_ref, o_ref, *, seq):
    ts = GQA_TQ // GQA_STREAMS
    rows = GROUP_A * ts
    qs, ms, ls, accs = [], [], [], []
    for t in range(GQA_STREAMS):
        tok = slice(t * ts, (t + 1) * ts)
        q = jnp.concatenate([q_ref[0, g, tok, :] for g in range(GROUP_A)], axis=0)
        qs.append(q.T)
        ms.append(jnp.full((1, rows), -jnp.inf, F32))
        ls.append(jnp.zeros((1, rows), F32))
        accs.append(jnp.zeros((HEAD_DIM, rows), F32))
    n_chunks = seq // GQA_TK

    def scores(c, t):
        k = k_ref[0, 0, c * GQA_TK:(c + 1) * GQA_TK, :]
        return jnp.dot(k, qs[t], preferred_element_type=F32)

    work = [(c, t) for c in range(n_chunks) for t in range(GQA_STREAMS)]
    st_next = scores(*work[0])
    for i, (c, t) in enumerate(work):
        st = st_next
        if i + 1 < len(work):
            st_next = scores(*work[i + 1])
        v = v_ref[0, 0, c * GQA_TK:(c + 1) * GQA_TK, :]
        m_new = jnp.maximum(ms[t], jnp.max(st, axis=0, keepdims=True))
        alpha = jnp.exp2(ms[t] - m_new)
        p = jnp.exp2(st - m_new)
        ls[t] = alpha * ls[t] + jnp.sum(p, axis=0, keepdims=True)
        pv = lax.dot_general(v, p.astype(BF16), (((0,), (0,)), ((), ())), preferred_element_type=F32)
        accs[t] = alpha * accs[t] + pv
        ms[t] = m_new
    for t in range(GQA_STREAMS):
        o = (accs[t] / ls[t]).T
        for g in range(GROUP_A):
            o_ref[0, t * ts:(t + 1) * ts, g * HEAD_DIM:(g + 1) * HEAD_DIM] = o[g * ts:(g + 1) * ts].astype(BF16)


def _gqa(qkv, batch, seq):
    gw = GROUP_A * HEAD_DIM
    return pl.pallas_call(
        functools.partial(_gqa_kernel, seq=seq),
        grid=(batch, N_KV_HEADS_A, seq // GQA_TQ),
        in_specs=[
            pl.BlockSpec((1, GROUP_A, GQA_TQ, HEAD_DIM), lambda b, h, i: (b, h, i, 0)),
            pl.BlockSpec((1, 1, seq, HEAD_DIM), lambda b, h, i: (b, CB_KA + h, 0, 0)),
            pl.BlockSpec((1, 1, seq, HEAD_DIM), lambda b, h, i: (b, CB_VA + h, 0, 0)),
        ],
        out_specs=pl.BlockSpec((1, GQA_TQ, gw), lambda b, h, i: (b, i, h)),
        out_shape=jax.ShapeDtypeStruct((batch, seq, WIDTH_QA), BF16),
        compiler_params=_params(3),
        name="gqa",
    )(qkv, qkv, qkv)


def _nbr_pair_table(rpb):
    qc = np.arange(GRID_W)[:, None]
    kc = np.arange(GRID_W)[None, :]
    cs = np.clip(qc - WIN_COLS // 2, 0, GRID_W - WIN_COLS)
    col_ok = (kc >= cs) & (kc < cs + WIN_COLS)
    dc = np.clip(kc - qc + WIN_COLS - 1, 0, 2 * WIN_COLS - 2)
    onehot = (dc.reshape(1, -1) == np.arange(2 * WIN_COLS - 1)[:, None]).astype(np.float32)
    n_dr = 2 * WIN_ROWS - 1
    picked = jnp.dot(rpb.astype(F32).reshape(N_HEADS_B * n_dr, -1), jnp.asarray(onehot),
                     precision=lax.Precision.HIGHEST).reshape(N_HEADS_B, n_dr, GRID_W, GRID_W)
    per_dr = jnp.where(col_ok[None, None], picked * LOG2E, NEG)
    pad = NBR_MAX_OFF - (WIN_ROWS - 1)
    blocked = jnp.full((N_HEADS_B, pad, GRID_W, GRID_W), NEG, F32)
    ext = jnp.concatenate([blocked, per_dr, blocked], axis=1)
    return jnp.concatenate([ext[:, :-1], ext[:, 1:]], axis=-1)


def _nbr_row_masks(rows):
    n_groups = rows // NBR_QR
    assert n_groups >= 3
    masks = []
    for gg in (0, 1, n_groups - 1):
        r = gg * NBR_QR + np.arange(NBR_QR)
        ws = int(np.clip(gg * NBR_QR - WIN_ROWS // 2, 0, rows - NBR_KR))
        rs = np.clip(r - WIN_ROWS // 2, 0, rows - WIN_ROWS)
        rk = ws + np.arange(NBR_KR)
        ok = (rk[None, :] >= rs[:, None]) & (rk[None, :] < rs[:, None] + WIN_ROWS)
        assert ok.sum(axis=1).min() == WIN_ROWS
        m = np.where(ok, 0.0, NEG).astype(np.float32)
        masks.append(np.repeat(np.repeat(m, GRID_W, axis=0), GRID_W, axis=1))
    return jnp.asarray(np.stack(masks))


def _nbr_kernel(q_ref, *refs, rows):
    k_refs = refs[:NBR_HEADS]
    v_refs = refs[NBR_HEADS:2 * NBR_HEADS]
    pair_ref, mask_ref, o_ref = refs[2 * NBR_HEADS:]
    rb = pl.program_id(2)
    n_groups = rows // NBR_QR
    tiles = []
    for g in range(NBR_GROUPS):
        gg = rb * NBR_GROUPS + g
        ws = jnp.clip(gg * NBR_QR - WIN_ROWS // 2, 0, rows - NBR_KR)
        pattern = jnp.where(gg == 0, 0, jnp.where(gg == n_groups - 1, 2, 1))
        d0 = ws - gg * NBR_QR + NBR_MAX_OFF
        start = pl.multiple_of(ws * GRID_W, GRID_W)
        qrows = slice(g * NBR_GQ, (g + 1) * NBR_GQ)
        for hh in range(NBR_HEADS):
            tiles.append((hh, qrows, start, d0, pattern))

    def scores(tile):
        hh, qrows, start, _, _ = tile
        k = k_refs[hh][0, 0, pl.ds(start, NBR_GK), :]
        return lax.dot_general(q_ref[0, hh, qrows, :], k, (((1,), (1,)), ((), ())), preferred_element_type=F32)

    s_next = scores(tiles[0])
    for i, (hh, qrows, start, d0, pattern) in enumerate(tiles):
        s = s_next
        if i + 1 < len(tiles):
            s_next = scores(tiles[i + 1])
        bias = jnp.concatenate(
            [jnp.concatenate([pair_ref[hh, d0 + 2 * c - rq] for c in range(NBR_KR // 2)], axis=1)
             for rq in range(NBR_QR)], axis=0)
        s = s + bias + mask_ref[pattern]
        m = jnp.max(s, axis=-1, keepdims=True)
        p = jnp.exp2(s - m)
        l = jnp.sum(p, axis=-1, keepdims=True)
        v = v_refs[hh][0, 0, pl.ds(start, NBR_GK), :]
        o = jnp.dot(p.astype(BF16), v, preferred_element_type=F32) / l
        o_ref[0, qrows, hh * HEAD_DIM:(hh + 1) * HEAD_DIM] = o.astype(BF16)


def _nbr(qkv, pair, rowmask, batch, seq):
    rows = seq // GRID_W
    n_off = pair.shape[1]
    hw = NBR_HEADS * HEAD_DIM

    def head_block(cb, hh):
        return pl.BlockSpec((1, 1, seq, HEAD_DIM), lambda b, h, i: (b, cb + NBR_HEADS * h + hh, 0, 0))

    return pl.pallas_call(
        functools.partial(_nbr_kernel, rows=rows),
        grid=(batch, N_HEADS_B // NBR_HEADS, rows // NBR_ROWS),
        in_specs=[pl.BlockSpec((1, NBR_HEADS, NBR_TQ, HEAD_DIM), lambda b, h, i: (b, CB_QB // NBR_HEADS + h, i, 0))]
        + [head_block(CB_KB, hh) for hh in range(NBR_HEADS)]
        + [head_block(CB_VB, hh) for hh in range(NBR_HEADS)]
        + [pl.BlockSpec((NBR_HEADS, n_off, GRID_W, 2 * GRID_W), lambda b, h, i: (h, 0, 0, 0)),
           pl.BlockSpec((3, NBR_GQ, NBR_GK), lambda b, h, i: (0, 0, 0))],
        out_specs=pl.BlockSpec((1, NBR_TQ, hw), lambda b, h, i: (b, i, h)),
        out_shape=jax.ShapeDtypeStruct((batch, seq, WIDTH_B), BF16),
        compiler_params=_params(3),
        name="nbr",
    )(qkv, *([qkv] * (2 * NBR_HEADS)), pair, rowmask)


def _merge_kernel(h_ref, ya_ref, yb_ref, gate_ref, wa_ref, wb_ref, wo_ref, o_ref):
    for r in range(0, MERGE_TM, MERGE_RC):
        rows = slice(r, r + MERGE_RC)
        ta = jnp.dot(ya_ref[rows, :], wa_ref[...], preferred_element_type=F32)
        tb = jnp.dot(yb_ref[rows, :], wb_ref[...], preferred_element_type=F32)
        merged = gate_ref[rows, :D_MODEL] * ta + gate_ref[rows, D_MODEL:] * tb
        o_ref[rows, :] = h_ref[rows, :] + jnp.dot(merged.astype(BF16), wo_ref[...], preferred_element_type=F32)


def _merge(h, ya, yb, gates, wa, wb, wo):
    n = h.shape[0]
    tm = MERGE_TM
    row = lambda i: (i, 0)
    fixed = lambda i: (0, 0)
    return pl.pallas_call(
        _merge_kernel,
        grid=(n // tm,),
        in_specs=[
            pl.BlockSpec((tm, D_MODEL), row),
            pl.BlockSpec((tm, WIDTH_QA), row),
            pl.BlockSpec((tm, WIDTH_B), row),
            pl.BlockSpec((tm, GATE_WIDTH), row),
            pl.BlockSpec((WIDTH_QA, D_MODEL), fixed, pipeline_mode=pl.Buffered(1)),
            pl.BlockSpec((WIDTH_B, D_MODEL), fixed, pipeline_mode=pl.Buffered(1)),
            pl.BlockSpec((D_MODEL, D_MODEL), fixed, pipeline_mode=pl.Buffered(1)),
        ],
        out_specs=pl.BlockSpec((tm, D_MODEL), row),
        out_shape=jax.ShapeDtypeStruct((n, D_MODEL), F32),
        compiler_params=_params(1),
        name="merge_out",
    )(h, ya, yb, gates, wa, wb, wo)


def _rope_tables(seq):
    t = np.arange(seq)
    row = (t // GRID_W).astype(np.float32)
    col = (t % GRID_W).astype(np.float32)
    n_pairs_axis = HEAD_DIM // 4
    inv = (ROPE_THETA ** (-np.arange(n_pairs_axis, dtype=np.float32) / n_pairs_axis)).astype(np.float32)
    ang = np.concatenate([row[:, None] * inv[None], col[:, None] * inv[None]], axis=-1)
    cos = np.cos(ang).astype(np.float32)
    sin = np.sin(ang).astype(np.float32)
    cos2 = np.concatenate([cos, cos], axis=-1)
    sin2 = np.concatenate([-sin, sin], axis=-1)
    return jnp.asarray(cos2), jnp.asarray(sin2)


def _ffn_weights(w_in, w_out):
    return _cast_cols(w_in, 2, FF_TILE, D_FF_PAD // FF_TILE), _cast_rows(w_out, D_FF_PAD)


def kernel(x_prompt, x_sample, g_ffn1, w_ffn1_in, w_ffn1_out, g_mix, w_in, b_gate, g_q_a, g_k_a, rpb_b,
           w_branch_a, w_branch_b, w_out, g_ffn2, w_ffn2_in, w_ffn2_out, g_final):
    assert g_ffn1.shape[0] == 1, "single-layer stack"
    w1 = _ffn_weights(w_ffn1_in[0], w_ffn1_out[0])
    w2 = _ffn_weights(w_ffn2_in[0], w_ffn2_out[0])
    head_order, perms = _rope_column_perms()
    w_proj = _cast_proj(w_in[0], perms)
    wa = w_branch_a[0].astype(BF16)
    wb = w_branch_b[0].astype(BF16)
    wo = w_out[0].astype(BF16)
    g1 = g_ffn1[0][None, :]
    gm = g_mix[0][None, :]
    g2 = g_ffn2[0][None, :]
    gf = g_final[None, :]
    gq = g_q_a[0][head_order][None, :]
    gk = g_k_a[0][head_order][None, :]
    bg = b_gate[0][None, :]
    pair = _nbr_pair_table(rpb_b[0])

    def trunk(x):
        batch, seq, _ = x.shape
        n = batch * seq
        cos2, sin2 = _rope_tables(seq)
        rowmask = _nbr_row_masks(seq // GRID_W)
        h, u = _ffn(x.reshape(n, D_MODEL), g1, *w1, gm, final=False)
        qkv = _proj(u, w_proj, gq, gk, cos2, sin2, batch, seq)
        gates = _gates(u, w_proj, bg, seq)
        ya = _gqa(qkv, batch, seq).reshape(n, WIDTH_QA)
        yb = _nbr(qkv, pair, rowmask, batch, seq).reshape(n, WIDTH_B)
        h2 = _merge(h, ya, yb, gates, wa, wb, wo)
        out = _ffn(h2, g2, *w2, gf, final=True)
        return out.reshape(batch, seq, D_MODEL)

    return (trunk(x_prompt), trunk(x_sample))
```

```python
import functools

import numpy as np
import jax
import jax.numpy as jnp
from jax import lax
from jax.experimental import pallas as pl
from jax.experimental.pallas import tpu as pltpu

D_MODEL = 2048
HEAD_DIM = 128
N_HEADS_A = 8
N_KV_HEADS_A = 2
GROUP_A = N_HEADS_A // N_KV_HEADS_A
N_HEADS_B = 8
GRID_W = 64
WIN_ROWS = 8
WIN_COLS = 16
D_FF = 5504
ROPE_THETA = 10000.0
EPS = 1e-6
NEG = -1e30

WIDTH_QA = N_HEADS_A * HEAD_DIM
WIDTH_KVA = N_KV_HEADS_A * HEAD_DIM
WIDTH_B = N_HEADS_B * HEAD_DIM
QKV_WIDTH = WIDTH_QA + 2 * WIDTH_KVA + 3 * WIDTH_B
GATE_WIDTH = 2 * D_MODEL
IN_WIDTH = QKV_WIDTH + GATE_WIDTH
N_QKV_HEADS = QKV_WIDTH // HEAD_DIM

CB_QA = 0
CB_KA = WIDTH_QA // HEAD_DIM
CB_VA = CB_KA + N_KV_HEADS_A
CB_QB = CB_VA + N_KV_HEADS_A
CB_KB = CB_QB + N_HEADS_B
CB_VB = CB_KB + N_HEADS_B

V7X_VMEM_BYTES = 64 * 1024 * 1024
VMEM_LIMIT = V7X_VMEM_BYTES - 2 * 1024 * 1024

FF_TILE = 512
FF_CHUNKS = -(-D_FF // FF_TILE)
FF_PER_STEP = 1
FF_STEPS = -(-FF_CHUNKS // FF_PER_STEP)
FF_LAST = FF_CHUNKS - FF_PER_STEP * (FF_STEPS - 1)
D_FF_PAD = FF_STEPS * FF_PER_STEP * FF_TILE
FFN_TM = 1024
CAST_ROWS = 128
CAST_COLS = 512
PROJ_TM = 2048
PROJ_TN = 512
PROJ_RC = 512
GATE_BLOCKS = 2
MERGE_TM = 512
MERGE_RC = 256
GQA_TQ = 512
GQA_STREAMS = 4
GQA_TK = 1024
NBR_ROWS = 32
NBR_QR = 4
NBR_GROUPS = NBR_ROWS // NBR_QR
NBR_KR = 12
NBR_HEADS = 4
NBR_TQ = NBR_ROWS * GRID_W
NBR_GQ = NBR_QR * GRID_W
NBR_GK = NBR_KR * GRID_W
NBR_MAX_OFF = NBR_KR - 1

LOG2E = float(np.log2(np.e))
QK_SCALE = HEAD_DIM ** -0.5 * LOG2E

F32 = jnp.float32
BF16 = jnp.bfloat16


def _params(n_axes):
    return pltpu.CompilerParams(dimension_semantics=("arbitrary",) * n_axes,
                                vmem_limit_bytes=VMEM_LIMIT)


def _rms(xf, g):
    ms = jnp.mean(xf * xf, axis=-1, keepdims=True)
    return xf * lax.rsqrt(ms + EPS) * g


def _cast_cols_kernel(w_ref, o_ref, *, valid):
    n_chunks, rows, tn = o_ref.shape
    for c in range(n_chunks):
        lo = c * tn
        width = max(0, min(tn, valid - lo))
        if width > 0:
            o_ref[c, :, :width] = w_ref[:, lo:lo + width].astype(BF16)
        if width < tn:
            o_ref[c, :, width:] = jnp.zeros((rows, tn - width), BF16)


def _cast_cols(w, n_parts, tn, n_chunks):
    r, total = w.shape
    valid = total // n_parts
    assert n_chunks * tn >= valid
    return pl.pallas_call(
        functools.partial(_cast_cols_kernel, valid=valid),
        grid=(n_parts, r // CAST_ROWS),
        in_specs=[pl.BlockSpec((CAST_ROWS, valid), lambda p, i: (i, p))],
        out_specs=pl.BlockSpec((None, n_chunks, CAST_ROWS, tn), lambda p, i: (p, 0, i, 0)),
        out_shape=jax.ShapeDtypeStruct((n_parts, n_chunks, r, tn), BF16),
        compiler_params=_params(2),
        name="cast_cols",
    )(w)


def _cast_proj_kernel(w_ref, perm_ref, o_ref):
    n_chunks, _, tn = o_ref.shape
    for c in range(n_chunks):
        cols = w_ref[:, c * tn:(c + 1) * tn].astype(BF16)
        if c < perm_ref.shape[0]:
            cols = jnp.dot(cols, perm_ref[c], preferred_element_type=F32).astype(BF16)
        o_ref[c] = cols


def _rope_column_perms():
    head = np.concatenate([np.arange(0, HEAD_DIM, 2), np.arange(1, HEAD_DIM, 2)])
    heads_per_tile = PROJ_TN // HEAD_DIM
    n_rope_heads = N_HEADS_A + N_KV_HEADS_A
    n_tiles = -(-n_rope_heads // heads_per_tile)
    mats = np.zeros((n_tiles, PROJ_TN, PROJ_TN), np.float32)
    for t in range(n_tiles):
        for hh in range(heads_per_tile):
            src = head if t * heads_per_tile + hh < n_rope_heads else np.arange(HEAD_DIM)
            mats[t, hh * HEAD_DIM + src, hh * HEAD_DIM + np.arange(HEAD_DIM)] = 1.0
    return head, jnp.asarray(mats, BF16)


def _cast_proj(w, perms):
    r, total = w.shape
    n_chunks = total // PROJ_TN
    return pl.pallas_call(
        _cast_proj_kernel,
        grid=(r // CAST_ROWS,),
        in_specs=[pl.BlockSpec((CAST_ROWS, total), lambda i: (i, 0)),
                  pl.BlockSpec(perms.shape, lambda i: (0, 0, 0))],
        out_specs=pl.BlockSpec((None, n_chunks, CAST_ROWS, PROJ_TN), lambda i: (0, 0, i, 0)),
        out_shape=jax.ShapeDtypeStruct((1, n_chunks, r, PROJ_TN), BF16),
        compiler_params=_params(1),
        name="cast_proj",
    )(w, perms)


def _cast_rows_kernel(w_ref, o_ref):
    r = w_ref.shape[0]
    o_ref[:r, :] = w_ref[...].astype(BF16)
    o_ref[r:, :] = jnp.zeros((o_ref.shape[0] - r, o_ref.shape[1]), BF16)


def _cast_rows(w, rows_out):
    r, c = w.shape
    return pl.pallas_call(
        _cast_rows_kernel,
        grid=(c // CAST_COLS,),
        in_specs=[pl.BlockSpec((r, CAST_COLS), lambda i: (0, i))],
        out_specs=pl.BlockSpec((rows_out, CAST_COLS), lambda i: (0, i)),
        out_shape=jax.ShapeDtypeStruct((rows_out, c), BF16),
        compiler_params=_params(1),
        name="cast_rows",
    )(w)


def _ffn_kernel(x_ref, g_ref, wgu_ref, wo_ref, g2_ref, *refs, final):
    if final:
        acc_ref, xn_ref = refs
    else:
        acc_ref, u_ref, xn_ref = refs
    j = pl.program_id(1)
    last = pl.num_programs(1) - 1

    @pl.when(j == 0)
    def _():
        xn_ref[...] = _rms(x_ref[...], g_ref[...]).astype(BF16)
        acc_ref[...] = jnp.zeros_like(acc_ref)

    def chunk(c):
        xn = xn_ref[...]
        gate = jnp.dot(xn, wgu_ref[0, c], preferred_element_type=F32)
        up = jnp.dot(xn, wgu_ref[1, c], preferred_element_type=F32)
        act = ((gate * jax.nn.sigmoid(gate)) * up).astype(BF16)
        acc_ref[...] += jnp.dot(act, wo_ref[c * FF_TILE:(c + 1) * FF_TILE, :], preferred_element_type=F32)

    @pl.when(j < last)
    def _():
        for c in range(FF_PER_STEP):
            chunk(c)

    @pl.when(j == last)
    def _():
        for c in range(FF_LAST):
            chunk(c)
        h = x_ref[...] + 0.5 * acc_ref[...]
        if final:
            acc_ref[...] = _rms(h, g2_ref[...])
        else:
            acc_ref[...] = h
            u_ref[...] = _rms(h, g2_ref[...]).astype(BF16)


def _ffn(x, g, w_gu, wo, g2, *, final):
    n = x.shape[0]
    tm = FFN_TM
    row = lambda i, j: (i, 0)
    in_specs = [
        pl.BlockSpec((tm, D_MODEL), row),
        pl.BlockSpec((1, D_MODEL), lambda i, j: (0, 0)),
        pl.BlockSpec((2, FF_PER_STEP, D_MODEL, FF_TILE), lambda i, j: (0, j, 0, 0)),
        pl.BlockSpec((FF_PER_STEP * FF_TILE, D_MODEL), lambda i, j: (j, 0)),
        pl.BlockSpec((1, D_MODEL), lambda i, j: (0, 0)),
    ]
    if final:
        out_shape = jax.ShapeDtypeStruct((n, D_MODEL), F32)
        out_specs = pl.BlockSpec((tm, D_MODEL), row)
    else:
        out_shape = (jax.ShapeDtypeStruct((n, D_MODEL), F32), jax.ShapeDtypeStruct((n, D_MODEL), BF16))
        out_specs = (pl.BlockSpec((tm, D_MODEL), row), pl.BlockSpec((tm, D_MODEL), row))
    return pl.pallas_call(
        functools.partial(_ffn_kernel, final=final),
        grid=(n // tm, FF_STEPS),
        in_specs=in_specs,
        out_specs=out_specs,
        out_shape=out_shape,
        scratch_shapes=[pltpu.VMEM((tm, D_MODEL), BF16)],
        compiler_params=_params(2),
        name="ffn_final" if final else "ffn_first",
    )(x, g, w_gu, wo, g2)


def _rope(y, cos2, sin2):
    return y * cos2 + pltpu.roll(y, HEAD_DIM // 2, 1) * sin2


def _row_chunks(n_rows):
    return [slice(r, r + PROJ_RC) for r in range(0, n_rows, PROJ_RC)]


def _proj_kernel(u_ref, w_ref, gq_ref, gk_ref, cos_ref, sin_ref, o_ref):
    j = pl.program_id(1)
    heads_per_tile = PROJ_TN // HEAD_DIM
    qa_tiles = WIDTH_QA // PROJ_TN
    kva_tile = qa_tiles
    qb_lo = (CB_QB * HEAD_DIM) // PROJ_TN
    qb_hi = (CB_KB * HEAD_DIM) // PROJ_TN

    def normed_rope(yh, g_ref, rows):
        return _rope(_rms(yh, g_ref[...]), cos_ref[rows, :], sin_ref[rows, :])

    def for_heads(epilogue):
        for rows in _row_chunks(u_ref.shape[0]):
            y = jnp.dot(u_ref[rows, :], w_ref[...], preferred_element_type=F32)
            for hh in range(heads_per_tile):
                o_ref[0, hh, rows, :] = epilogue(hh, rows, y[:, hh * HEAD_DIM:(hh + 1) * HEAD_DIM]).astype(BF16)

    @pl.when(j < qa_tiles)
    def _():
        for_heads(lambda hh, rows, yh: normed_rope(yh, gq_ref, rows) * QK_SCALE)

    @pl.when(j == kva_tile)
    def _():
        for_heads(lambda hh, rows, yh: normed_rope(yh, gk_ref, rows) if hh < N_KV_HEADS_A else yh)

    @pl.when((j >= qb_lo) & (j < qb_hi))
    def _():
        for_heads(lambda hh, rows, yh: yh * QK_SCALE)

    @pl.when(j >= qb_hi)
    def _():
        for_heads(lambda hh, rows, yh: yh)


def _proj(u, w_blocks, gq, gk, cos2, sin2, batch, seq):
    n = u.shape[0]
    tm = min(PROJ_TM, seq)
    pos_tiles = seq // tm
    heads_per_tile = PROJ_TN // HEAD_DIM
    return pl.pallas_call(
        _proj_kernel,
        grid=(n // tm, QKV_WIDTH // PROJ_TN),
        in_specs=[
            pl.BlockSpec((tm, D_MODEL), lambda i, j: (i, 0)),
            pl.BlockSpec((None, None, D_MODEL, PROJ_TN), lambda i, j: (0, j, 0, 0)),
            pl.BlockSpec((1, HEAD_DIM), lambda i, j: (0, 0)),
            pl.BlockSpec((1, HEAD_DIM), lambda i, j: (0, 0)),
            pl.BlockSpec((tm, HEAD_DIM), lambda i, j: (i % pos_tiles, 0)),
            pl.BlockSpec((tm, HEAD_DIM), lambda i, j: (i % pos_tiles, 0)),
        ],
        out_specs=pl.BlockSpec((1, heads_per_tile, tm, HEAD_DIM),
                               lambda i, j: (i // pos_tiles, j, i % pos_tiles, 0)),
        out_shape=jax.ShapeDtypeStruct((batch, N_QKV_HEADS, seq, HEAD_DIM), BF16),
        compiler_params=_params(2),
        name="proj_qkv",
    )(u, w_blocks, gq, gk, cos2, sin2)


def _gate_kernel(u_ref, *refs):
    w_refs, (b_ref, o_ref) = refs[:GATE_BLOCKS], refs[GATE_BLOCKS:]
    for rows in _row_chunks(u_ref.shape[0]):
        for c, w_ref in enumerate(w_refs):
            cols = slice(c * PROJ_TN, (c + 1) * PROJ_TN)
            y = jnp.dot(u_ref[rows, :], w_ref[...], preferred_element_type=F32)
            o_ref[rows, cols] = 0.5 * jnp.tanh(0.5 * (y + b_ref[:, cols])) + 0.5


def _gates(u, w_blocks, b_gate, seq):
    n = u.shape[0]
    tm = min(PROJ_TM, seq)
    first = QKV_WIDTH // PROJ_TN
    width = GATE_BLOCKS * PROJ_TN

    def chunk(c):
        return pl.BlockSpec((None, None, D_MODEL, PROJ_TN), lambda i, j: (0, first + GATE_BLOCKS * j + c, 0, 0))

    return pl.pallas_call(
        _gate_kernel,
        grid=(n // tm, GATE_WIDTH // width),
        in_specs=[pl.BlockSpec((tm, D_MODEL), lambda i, j: (i, 0))]
        + [chunk(c) for c in range(GATE_BLOCKS)]
        + [pl.BlockSpec((1, width), lambda i, j: (0, j))],
        out_specs=pl.BlockSpec((tm, width), lambda i, j: (i, j)),
        out_shape=jax.ShapeDtypeStruct((n, GATE_WIDTH), F32),
        compiler_params=_params(2),
        name="proj_gates",
    )(u, *([w_blocks] * GATE_BLOCKS), b_gate)


def _gqa_kernel(q_ref, k_ref, v_ref, o_ref, *, seq):
    ts = GQA_TQ // GQA_STREAMS
    rows = GROUP_A * ts
    qs, ms, ls, accs = [], [], [], []
    for t in range(GQA_STREAMS):
        tok = slice(t * ts, (t + 1) * ts)
        q = jnp.concatenate([q_ref[0, g, tok, :] for g in range(GROUP_A)], axis=0)
        qs.append(q.T)
        ms.append(jnp.full((1, rows), -jnp.inf, F32))
        ls.append(jnp.zeros((1, rows), F32))
        accs.append(jnp.zeros((HEAD_DIM, rows), F32))
    n_chunks = seq // GQA_TK

    def scores(c, t):
        k = k_ref[0, 0, c * GQA_TK:(c + 1) * GQA_TK, :]
        return jnp.dot(k, qs[t], preferred_element_type=F32)

    work = [(c, t) for c in range(n_chunks) for t in range(GQA_STREAMS)]
    st_next = scores(*work[0])
    for i, (c, t) in enumerate(work):
        st = st_next
        if i + 1 < len(work):
            st_next = scores(*work[i + 1])
        v = v_ref[0, 0, c * GQA_TK:(c + 1) * GQA_TK, :]
        m_new = jnp.maximum(ms[t], jnp.max(st, axis=0, keepdims=True))
        alpha = jnp.exp2(ms[t] - m_new)
        p = jnp.exp2(st - m_new)
        ls[t] = alpha * ls[t] + jnp.sum(p, axis=0, keepdims=True)
        pv = lax.dot_general(v, p.astype(BF16), (((0,), (0,)), ((), ())), preferred_element_type=F32)
        accs[t] = alpha * accs[t] + pv
        ms[t] = m_new
    for t in range(GQA_STREAMS):
        o = (accs[t] / ls[t]).T
        for g in range(GROUP_A):
            o_ref[0, t * ts:(t + 1) * ts, g * HEAD_DIM:(g + 1) * HEAD_DIM] = o[g * ts:(g + 1) * ts].astype(BF16)


def _gqa(qkv, batch, seq):
    gw = GROUP_A * HEAD_DIM
    return pl.pallas_call(
        functools.partial(_gqa_kernel, seq=seq),
        grid=(batch, N_KV_HEADS_A, seq // GQA_TQ),
        in_specs=[
            pl.BlockSpec((1, GROUP_A, GQA_TQ, HEAD_DIM), lambda b, h, i: (b, h, i, 0)),
            pl.BlockSpec((1, 1, seq, HEAD_DIM), lambda b, h, i: (b, CB_KA + h, 0, 0)),
            pl.BlockSpec((1, 1, seq, HEAD_DIM), lambda b, h, i: (b, CB_VA + h, 0, 0)),
        ],
        out_specs=pl.BlockSpec((1, GQA_TQ, gw), lambda b, h, i: (b, i, h)),
        out_shape=jax.ShapeDtypeStruct((batch, seq, WIDTH_QA), BF16),
        compiler_params=_params(3),
        name="gqa",
    )(qkv, qkv, qkv)


def _nbr_pair_table(rpb):
    qc = np.arange(GRID_W)[:, None]
    kc = np.arange(GRID_W)[None, :]
    cs = np.clip(qc - WIN_COLS // 2, 0, GRID_W - WIN_COLS)
    col_ok = (kc >= cs) & (kc < cs + WIN_COLS)
    dc = np.clip(kc - qc + WIN_COLS - 1, 0, 2 * WIN_COLS - 2)
    onehot = (dc.reshape(1, -1) == np.arange(2 * WIN_COLS - 1)[:, None]).astype(np.float32)
    n_dr = 2 * WIN_ROWS - 1
    picked = jnp.dot(rpb.astype(F32).reshape(N_HEADS_B * n_dr, -1), jnp.asarray(onehot),
                     precision=lax.Precision.HIGHEST).reshape(N_HEADS_B, n_dr, GRID_W, GRID_W)
    per_dr = jnp.where(col_ok[None, None], picked * LOG2E, NEG)
    pad = NBR_MAX_OFF - (WIN_ROWS - 1)
    blocked = jnp.full((N_HEADS_B, pad, GRID_W, GRID_W), NEG, F32)
    ext = jnp.concatenate([blocked, per_dr, blocked], axis=1)
    return jnp.concatenate([ext[:, :-1], ext[:, 1:]], axis=-1)


def _nbr_row_masks(rows):
    n_groups = rows // NBR_QR
    assert n_groups >= 3
    masks = []
    for gg in (0, 1, n_groups - 1):
        r = gg * NBR_QR + np.arange(NBR_QR)
        ws = int(np.clip(gg * NBR_QR - WIN_ROWS // 2, 0, rows - NBR_KR))
        rs = np.clip(r - WIN_ROWS // 2, 0, rows - WIN_ROWS)
        rk = ws + np.arange(NBR_KR)
        ok = (rk[None, :] >= rs[:, None]) & (rk[None, :] < rs[:, None] + WIN_ROWS)
        assert ok.sum(axis=1).min() == WIN_ROWS
        m = np.where(ok, 0.0, NEG).astype(np.float32)
        masks.append(np.repeat(np.repeat(m, GRID_W, axis=0), GRID_W, axis=1))
    return jnp.asarray(np.stack(masks))


def _nbr_kernel(q_ref, *refs, rows):
    k_refs = refs[:NBR_HEADS]
    v_refs = refs[NBR_HEADS:2 * NBR_HEADS]
    pair_ref, mask_ref, o_ref = refs[2 * NBR_HEADS:]
    rb = pl.program_id(2)
    n_groups = rows // NBR_QR
    tiles = []
    for g in range(NBR_GROUPS):
        gg = rb * NBR_GROUPS + g
        ws = jnp.clip(gg * NBR_QR - WIN_ROWS // 2, 0, rows - NBR_KR)
        pattern = jnp.where(gg == 0, 0, jnp.where(gg == n_groups - 1, 2, 1))
        d0 = ws - gg * NBR_QR + NBR_MAX_OFF
        start = pl.multiple_of(ws * GRID_W, GRID_W)
        qrows = slice(g * NBR_GQ, (g + 1) * NBR_GQ)
        for hh in range(NBR_HEADS):
            tiles.append((hh, qrows, start, d0, pattern))

    def scores(tile):
        hh, qrows, start, _, _ = tile
        k = k_refs[hh][0, 0, pl.ds(start, NBR_GK), :]
        return lax.dot_general(q_ref[0, hh, qrows, :], k, (((1,), (1,)), ((), ())), preferred_element_type=F32)

    s_next = scores(tiles[0])
    for i, (hh, qrows, start, d0, pattern) in enumerate(tiles):
        s = s_next
        if i + 1 < len(tiles):
            s_next = scores(tiles[i + 1])
        bias = jnp.concatenate(
            [jnp.concatenate([pair_ref[hh, d0 + 2 * c - rq] for c in range(NBR_KR // 2)], axis=1)
             for rq in range(NBR_QR)], axis=0)
        s = s + bias + mask_ref[pattern]
        m = jnp.max(s, axis=-1, keepdims=True)
        p = jnp.exp2(s - m)
        l = jnp.sum(p, axis=-1, keepdims=True)
        v = v_refs[hh][0, 0, pl.ds(start, NBR_GK), :]
        o = jnp.dot(p.astype(BF16), v, preferred_element_type=F32) / l
        o_ref[0, qrows, hh * HEAD_DIM:(hh + 1) * HEAD_DIM] = o.astype(BF16)


def _nbr(qkv, pair, rowmask, batch, seq):
    rows = seq // GRID_W
    n_off = pair.shape[1]
    hw = NBR_HEADS * HEAD_DIM

    def head_block(cb, hh):
        return pl.BlockSpec((1, 1, seq, HEAD_DIM), lambda b, h, i: (b, cb + NBR_HEADS * h + hh, 0, 0))

    return pl.pallas_call(
        functools.partial(_nbr_kernel, rows=rows),
        grid=(batch, N_HEADS_B // NBR_HEADS, rows // NBR_ROWS),
        in_specs=[pl.BlockSpec((1, NBR_HEADS, NBR_TQ, HEAD_DIM), lambda b, h, i: (b, CB_QB // NBR_HEADS + h, i, 0))]
        + [head_block(CB_KB, hh) for hh in range(NBR_HEADS)]
        + [head_block(CB_VB, hh) for hh in range(NBR_HEADS)]
        + [pl.BlockSpec((NBR_HEADS, n_off, GRID_W, 2 * GRID_W), lambda b, h, i: (h, 0, 0, 0)),
           pl.BlockSpec((3, NBR_GQ, NBR_GK), lambda b, h, i: (0, 0, 0))],
        out_specs=pl.BlockSpec((1, NBR_TQ, hw), lambda b, h, i: (b, i, h)),
        out_shape=jax.ShapeDtypeStruct((batch, seq, WIDTH_B), BF16),
        compiler_params=_params(3),
        name="nbr",
    )(qkv, *([qkv] * (2 * NBR_HEADS)), pair, rowmask)


def _merge_kernel(h_ref, ya_ref, yb_ref, gate_ref, wa_ref, wb_ref, wo_ref, o_ref):
    for r in range(0, MERGE_TM, MERGE_RC):
        rows = slice(r, r + MERGE_RC)
        ta = jnp.dot(ya_ref[rows, :], wa_ref[...], preferred_element_type=F32)
        tb = jnp.dot(yb_ref[rows, :], wb_ref[...], preferred_element_type=F32)
        merged = gate_ref[rows, :D_MODEL] * ta + gate_ref[rows, D_MODEL:] * tb
        o_ref[rows, :] = h_ref[rows, :] + jnp.dot(merged.astype(BF16), wo_ref[...], preferred_element_type=F32)


def _merge(h, ya, yb, gates, wa, wb, wo):
    n = h.shape[0]
    tm = MERGE_TM
    row = lambda i: (i, 0)
    fixed = lambda i: (0, 0)
    return pl.pallas_call(
        _merge_kernel,
        grid=(n // tm,),
        in_specs=[
            pl.BlockSpec((tm, D_MODEL), row),
            pl.BlockSpec((tm, WIDTH_QA), row),
            pl.BlockSpec((tm, WIDTH_B), row),
            pl.BlockSpec((tm, GATE_WIDTH), row),
            pl.BlockSpec((WIDTH_QA, D_MODEL), fixed, pipeline_mode=pl.Buffered(1)),
            pl.BlockSpec((WIDTH_B, D_MODEL), fixed, pipeline_mode=pl.Buffered(1)),
            pl.BlockSpec((D_MODEL, D_MODEL), fixed, pipeline_mode=pl.Buffered(1)),
        ],
        out_specs=pl.BlockSpec((tm, D_MODEL), row),
        out_shape=jax.ShapeDtypeStruct((n, D_MODEL), F32),
        compiler_params=_params(1),
        name="merge_out",
    )(h, ya, yb, gates, wa, wb, wo)


def _rope_tables(seq):
    t = np.arange(seq)
    row = (t // GRID_W).astype(np.float32)
    col = (t % GRID_W).astype(np.float32)
    n_pairs_axis = HEAD_DIM // 4
    inv = (ROPE_THETA ** (-np.arange(n_pairs_axis, dtype=np.float32) / n_pairs_axis)).astype(np.float32)
    ang = np.concatenate([row[:, None] * inv[None], col[:, None] * inv[None]], axis=-1)
    cos = np.cos(ang).astype(np.float32)
    sin = np.sin(ang).astype(np.float32)
    cos2 = np.concatenate([cos, cos], axis=-1)
    sin2 = np.concatenate([-sin, sin], axis=-1)
    return jnp.asarray(cos2), jnp.asarray(sin2)


def _ffn_weights(w_in, w_out):
    return _cast_cols(w_in, 2, FF_TILE, D_FF_PAD // FF_TILE), _cast_rows(w_out, D_FF_PAD)


def kernel(x_prompt, x_sample, g_ffn1, w_ffn1_in, w_ffn1_out, g_mix, w_in, b_gate, g_q_a, g_k_a, rpb_b,
           w_branch_a, w_branch_b, w_out, g_ffn2, w_ffn2_in, w_ffn2_out, g_final):
    assert g_ffn1.shape[0] == 1, "single-layer stack"
    w1 = _ffn_weights(w_ffn1_in[0], w_ffn1_out[0])
    w2 = _ffn_weights(w_ffn2_in[0], w_ffn2_out[0])
    head_order, perms = _rope_column_perms()
    w_proj = _cast_proj(w_in[0], perms)
    wa = w_branch_a[0].astype(BF16)
    wb = w_branch_b[0].astype(BF16)
    wo = w_out[0].astype(BF16)
    g1 = g_ffn1[0][None, :]
    gm = g_mix[0][None, :]
    g2 = g_ffn2[0][None, :]
    gf = g_final[None, :]
    gq = g_q_a[0][head_order][None, :]
    gk = g_k_a[0][head_order][None, :]
    bg = b_gate[0][None, :]
    pair = _nbr_pair_table(rpb_b[0])

    def trunk(x):
        batch, seq, _ = x.shape
        n = batch * seq
        cos2, sin2 = _rope_tables(seq)
        rowmask = _nbr_row_masks(seq // GRID_W)
        h, u = _ffn(x.reshape(n, D_MODEL), g1, *w1, gm, final=False)
        qkv = _proj(u, w_proj, gq, gk, cos2, sin2, batch, seq)
        gates = _gates(u, w_proj, bg, seq)
        ya = _gqa(qkv, batch, seq).reshape(n, WIDTH_QA)
        yb = _nbr(qkv, pair, rowmask, batch, seq).reshape(n, WIDTH_B)
        h2 = _merge(h, ya, yb, gates, wa, wb, wo)
        out = _ffn(h2, g2, *w2, gf, final=True)
        return out.reshape(batch, seq, D_MODEL)

    return (trunk(x_prompt), trunk(x_sample))
```
